```python
import jax, jax.numpy as jnp
from jax import lax
import numpy as np

D_MODEL = 4096
BATCH = 8
SEQ = 2048
DEPTH = 4

HEAD_DIM = 128
DILATED_CFG = ((128, 1), (512, 4), (2048, 16))
N_DIL = len(DILATED_CFG)
A_HEADS_PER_GROUP = D_MODEL // 1024
A_HEADS = N_DIL * A_HEADS_PER_GROUP
A_W = A_HEADS * HEAD_DIM
A_OUT = A_HEADS_PER_GROUP * HEAD_DIM
CONV_W = D_MODEL // 4
CONV_K = 31
FOX_HEADS = D_MODEL // 512
FOX_W = FOX_HEADS * HEAD_DIM
FOX_BLOCK = 128
POOL_SIZES = (2, 4, 8, 16)
N_POOL = len(POOL_SIZES)
POOL_W = D_MODEL // 4
POOL_GC = POOL_W // N_POOL
EPS = 1e-6

IN_LAYOUT = (
    ("a_q", A_W), ("a_k", A_W), ("a_v", A_W), ("a_gate", A_OUT),
    ("b_glu", 2 * CONV_W), ("b_gate", CONV_W),
    ("c_q", FOX_W), ("c_k", FOX_W), ("c_v", FOX_W), ("c_f", FOX_HEADS), ("c_gate", FOX_W),
    ("d_in", POOL_W), ("d_gate", POOL_W),
    ("g_a", D_MODEL), ("g_b", D_MODEL), ("g_c", D_MODEL), ("g_d", D_MODEL),
)
N_IN = sum(w for _, w in IN_LAYOUT)
BRANCH_W = (A_OUT, CONV_W, FOX_W, POOL_W)
BRANCH_ROWS = sum(BRANCH_W)

kernel_name = "hybrid_gated_dilated_conv_fox_pool"


def rmsnorm(x, g):
    xf = x.astype(jnp.float32)
    y = xf * lax.rsqrt(jnp.mean(xf * xf, axis=-1, keepdims=True) + EPS)
    return (y * g.astype(jnp.float32)).astype(x.dtype)


def in_proj(h, w_in, name):
    start = 0
    for n, width in IN_LAYOUT:
        if n == name:
            return h @ w_in[:, start:start + width]
        start += width
    raise KeyError(name)


def dilated_group(q, k, v, window, dilation):
    B, S, H, hd = q.shape
    blk = window // dilation
    L = S // dilation
    nb = -(-L // blk)
    Lp = nb * blk

    def to_sub(t):
        t = t.reshape(B, L, dilation, H, hd).transpose(0, 2, 3, 1, 4)
        t = jnp.pad(t, ((0, 0), (0, 0), (0, 0), (0, Lp - L), (0, 0)))
        return t.reshape(B, dilation, H, nb, blk, hd)

    def with_prev(t):
        prev = jnp.concatenate([jnp.zeros_like(t[:, :, :, :1]), t[:, :, :, :-1]], axis=3)
        return jnp.concatenate([prev, t], axis=4)

    qs, k2, v2 = to_sub(q), with_prev(to_sub(k)), with_prev(to_sub(v))
    s = jnp.einsum('bdhnqe,bdhnke->bdhnqk', qs, k2).astype(jnp.float32) * (hd ** -0.5)
    qi = jnp.arange(blk)[:, None]
    kc = jnp.arange(2 * blk)[None, :]
    dist = qi + blk - kc
    band = (dist >= 0) & (dist <= blk)
    has_prev = (jnp.arange(nb)[:, None, None] > 0) | (kc >= blk)[None]
    mask = band[None] & has_prev
    s = jnp.where(mask, s, -jnp.inf)
    lse = jax.nn.logsumexp(s, axis=-1)
    p = jnp.exp(s - lse[..., None])
    o = jnp.einsum('bdhnqk,bdhnke->bdhnqe', p.astype(v.dtype), v2)
    o = o.reshape(B, dilation, H, Lp, hd)[:, :, :, :L].transpose(0, 3, 1, 2, 4).reshape(B, S, H, hd)
    lse = lse.reshape(B, dilation, H, Lp)[..., :L].transpose(0, 3, 1, 2).reshape(B, S, H)
    return o, lse


def mixer_dilated(h, w_in):
    B, S, _ = h.shape
    q = in_proj(h, w_in, "a_q").reshape(B, S, A_HEADS, HEAD_DIM)
    k = in_proj(h, w_in, "a_k").reshape(B, S, A_HEADS, HEAD_DIM)
    v = in_proj(h, w_in, "a_v").reshape(B, S, A_HEADS, HEAD_DIM)
    outs, lses = [], []
    for g, (window, dil) in enumerate(DILATED_CFG):
        sl = slice(g * A_HEADS_PER_GROUP, (g + 1) * A_HEADS_PER_GROUP)
        o, l = dilated_group(q[:, :, sl], k[:, :, sl], v[:, :, sl], window, dil)
        outs.append(o)
        lses.append(l)
    wts = jax.nn.softmax(jnp.stack(lses, axis=0), axis=0)
    o = jnp.sum(wts[..., None].astype(outs[0].dtype) * jnp.stack(outs, axis=0), axis=0)
    return o.reshape(B, S, A_OUT)


def mixer_conformer_conv(h, w_in, conv_w, conv_b, ln_g, ln_b, pw, pw_b):
    u = in_proj(h, w_in, "b_glu")
    a, b = jnp.split(u, 2, axis=-1)
    y = a * jax.nn.sigmoid(b)
    y = lax.conv_general_dilated(
        y, conv_w[:, None, :].astype(y.dtype), window_strides=(1,),
        padding=[(CONV_K - 1, 0)], dimension_numbers=('NWC', 'WIO', 'NWC'),
        feature_group_count=CONV_W) + conv_b
    yf = y.astype(jnp.float32)
    mu = jnp.mean(yf, axis=-1, keepdims=True)
    var = jnp.mean(jnp.square(yf - mu), axis=-1, keepdims=True)
    yf = (yf - mu) * lax.rsqrt(var + EPS) * ln_g.astype(jnp.float32) + ln_b.astype(jnp.float32)
    y = jax.nn.silu(yf).astype(h.dtype)
    return y @ pw + pw_b


def mixer_fox(h, w_in, b_forget):
    B, S, _ = h.shape
    def heads(t):
        return t.reshape(B, S, FOX_HEADS, HEAD_DIM).transpose(0, 2, 1, 3)
    q = heads(in_proj(h, w_in, "c_q"))
    k = heads(in_proj(h, w_in, "c_k"))
    v = heads(in_proj(h, w_in, "c_v"))
    f_logit = (in_proj(h, w_in, "c_f") + b_forget).astype(jnp.float32)
    c = jnp.cumsum(jax.nn.log_sigmoid(f_logit), axis=1).transpose(0, 2, 1)
    nq = S // FOX_BLOCK
    qb = q.reshape(B, FOX_HEADS, nq, FOX_BLOCK, HEAD_DIM).transpose(2, 0, 1, 3, 4)
    cb = c.reshape(B, FOX_HEADS, nq, FOX_BLOCK).transpose(2, 0, 1, 3)
    starts = jnp.arange(nq) * FOX_BLOCK
    kpos = jnp.arange(S)
    scale = HEAD_DIM ** -0.5

    def one_block(args):
        qj, cj, st = args
        s = jnp.einsum('bhqe,bhke->bhqk', qj, k).astype(jnp.float32) * scale
        s = s + cj[..., None] - c[:, :, None, :]
        qpos = st + jnp.arange(FOX_BLOCK)
        s = jnp.where(kpos[None, :] <= qpos[:, None], s, -jnp.inf)
        p = jax.nn.softmax(s, axis=-1)
        return jnp.einsum('bhqk,bhke->bhqe', p.astype(v.dtype), v)

    o = lax.map(one_block, (qb, cb, starts))
    return o.transpose(1, 0, 3, 2, 4).reshape(B, S, FOX_W)


def mixer_pool(h, w_in, pool_w, pool_scale):
    B, S, _ = h.shape
    xd = in_proj(h, w_in, "d_in")
    xf = xd.astype(jnp.float32)
    t = jnp.arange(S)
    diffs = []
    for g, p in enumerate(POOL_SIZES):
        xg = xf[..., g * POOL_GC:(g + 1) * POOL_GC]
        cs = jnp.cumsum(xg, axis=1)
        lagged = jnp.pad(cs[:, :S - p], ((0, 0), (p, 0), (0, 0)))
        cnt = jnp.minimum(t + 1, p).astype(jnp.float32)[None, :, None]
        diffs.append((cs - lagged) / cnt - xg)
    d = jnp.stack(diffs, axis=2).astype(h.dtype)
    y = jnp.einsum('bsgc,gcd->bsgd', d, pool_w).reshape(B, S, POOL_W)
    return y * pool_scale


def setup_inputs(seed: int = 0) -> dict:
    key = jax.random.key(seed)
    ks = jax.random.split(key, 16)
    f32 = jnp.float32
    row_scale = jnp.concatenate([jnp.full((w,), w ** -0.5, f32) for w in BRANCH_W])
    return {
        "x": jax.random.normal(ks[0], (BATCH, SEQ, D_MODEL), f32),
        "norm_g": 1.0 + 0.02 * jax.random.normal(ks[1], (DEPTH, D_MODEL), f32),
        "w_in": jax.random.normal(ks[2], (DEPTH, D_MODEL, N_IN), f32) * (D_MODEL ** -0.5),
        "b_forget": 2.0 + 0.5 * jax.random.normal(ks[3], (DEPTH, FOX_HEADS), f32),
        "conv_w": jax.random.normal(ks[4], (DEPTH, CONV_K, CONV_W), f32) * (CONV_K ** -0.5),
        "conv_b": 0.01 * jax.random.normal(ks[5], (DEPTH, CONV_W), f32),
        "conv_ln_g": 1.0 + 0.02 * jax.random.normal(ks[6], (DEPTH, CONV_W), f32),
        "conv_ln_b": 0.01 * jax.random.normal(ks[7], (DEPTH, CONV_W), f32),
        "conv_pw": jax.random.normal(ks[8], (DEPTH, CONV_W, CONV_W), f32) * (CONV_W ** -0.5),
        "conv_pw_b": 0.01 * jax.random.normal(ks[9], (DEPTH, CONV_W), f32),
        "pool_w": jax.random.normal(ks[10], (DEPTH, N_POOL, POOL_GC, POOL_GC), f32) * (POOL_GC ** -0.5),
        "pool_scale": 1.0 + 0.02 * jax.random.normal(ks[11], (DEPTH, POOL_W), f32),
        "w_branch": jax.random.normal(ks[12], (DEPTH, BRANCH_ROWS, D_MODEL), f32) * row_scale[None, :, None],
        "w_out": jax.random.normal(ks[13], (DEPTH, D_MODEL, D_MODEL), f32) * (D_MODEL ** -0.5),
        "final_g": 1.0 + 0.02 * jax.random.normal(ks[14], (D_MODEL,), f32),
    }


def reference(x, norm_g, w_in, b_forget, conv_w, conv_b, conv_ln_g, conv_ln_b,
              conv_pw, conv_pw_b, pool_w, pool_scale, w_branch, w_out, final_g):
    row_off = [0]
    for w in BRANCH_W:
        row_off.append(row_off[-1] + w)
    for l in range(DEPTH):
        h = rmsnorm(x, norm_g[l])
        wi = w_in[l]
        y_a = mixer_dilated(h, wi) * jax.nn.silu(in_proj(h, wi, "a_gate"))
        y_b = mixer_conformer_conv(h, wi, conv_w[l], conv_b[l], conv_ln_g[l], conv_ln_b[l],
                                   conv_pw[l], conv_pw_b[l]) * jax.nn.silu(in_proj(h, wi, "b_gate"))
        y_c = mixer_fox(h, wi, b_forget[l]) * jax.nn.silu(in_proj(h, wi, "c_gate"))
        y_d = mixer_pool(h, wi, pool_w[l], pool_scale[l]) * jax.nn.silu(in_proj(h, wi, "d_gate"))
        merged = jnp.zeros_like(x)
        for i, (y, gname) in enumerate(((y_a, "g_a"), (y_b, "g_b"), (y_c, "g_c"), (y_d, "g_d"))):
            u = y @ w_branch[l, row_off[i]:row_off[i + 1]]
            merged = merged + jax.nn.sigmoid(in_proj(h, wi, gname)) * u
        x = x + merged @ w_out[l]
    return rmsnorm(x, final_g)
```

```python
import functools
from typing import NamedTuple

import jax
import jax.numpy as jnp
from jax import lax
from jax.experimental import pallas as pl
from jax.experimental.pallas import tpu as pltpu

F32 = jnp.float32
BF16 = jnp.bfloat16

EPS = 1e-6
HEAD_DIM = 128
LANE = 128
DILATED_CFG = ((128, 1), (512, 4), (2048, 16))
BAND = 128
CONV_K = 31
CONV_HALO = 32
POOL_SIZES = (2, 4, 8, 16)
POOL_HALO = 16
VMEM_LIMIT = 56 * 1024 * 1024


class _Dims(NamedTuple):
    batch: int
    seq: int
    d_model: int
    hpg: int
    w4: int
    fox_heads: int
    cb_aq: int
    cb_ak: int
    cb_av: int
    cb_ag: int
    cb_ba: int
    cb_bb: int
    cb_bg: int
    cb_cq: int
    cb_ck: int
    cb_cv: int
    cb_cg: int
    cb_di: int
    cb_dg: int
    cb_gate: int
    n_blocks: int
    cf_col: int


def _dims(x_shape):
    batch, seq, d = x_shape
    hpg = d // 1024
    w4 = d // 512
    fox_heads = d // 512
    cb_aq = 0
    cb_ak = 3 * hpg
    cb_av = 6 * hpg
    cb_ag = 9 * hpg
    cb_ba = 10 * hpg
    cb_bb = cb_ba + w4
    cb_bg = cb_bb + w4
    cb_cq = cb_bg + w4
    cb_ck = cb_cq + w4
    cb_cv = cb_ck + w4
    cb_cg = cb_cv + w4
    cb_di = cb_cg + w4
    cb_dg = cb_di + w4
    cb_gate = cb_dg + w4
    n_blocks = cb_gate + 4 * (d // LANE)
    return _Dims(batch, seq, d, hpg, w4, fox_heads, cb_aq, cb_ak, cb_av, cb_ag, cb_ba, cb_bb,
                 cb_bg, cb_cq, cb_ck, cb_cv, cb_cg, cb_di, cb_dg, cb_gate, n_blocks,
                 cb_cg * LANE)


def _params(*sem):
    return pltpu.CompilerParams(dimension_semantics=sem, vmem_limit_bytes=VMEM_LIMIT)


def _silu(x):
    return x * jax.nn.sigmoid(x)


def _rmsnorm_kernel(x_ref, g_ref, o_ref):
    x = x_ref[...]
    ms = jnp.mean(x * x, axis=-1, keepdims=True)
    o_ref[...] = (x * lax.rsqrt(ms + EPS) * g_ref[...]).astype(o_ref.dtype)


def _rmsnorm(x2, g, out_dtype):
    m, d = x2.shape
    tm = min(512, m)
    return pl.pallas_call(
        _rmsnorm_kernel,
        grid=(m // tm,),
        in_specs=[pl.BlockSpec((tm, d), lambda i: (i, 0)),
                  pl.BlockSpec((1, d), lambda i: (0, 0))],
        out_specs=pl.BlockSpec((tm, d), lambda i: (i, 0)),
        out_shape=jax.ShapeDtypeStruct((m, d), out_dtype),
        compiler_params=_params("parallel"),
        name="rmsnorm",
    )(x2, g.reshape(1, d))


def _inproj_kernel(h_ref, w_ref, o_ref, *, first_gate_tile):
    acc = jnp.dot(h_ref[...], w_ref[...], preferred_element_type=F32)
    j = pl.program_id(1)

    @pl.when(j < first_gate_tile)
    def _():
        o_ref[...] = acc.astype(o_ref.dtype)

    @pl.when(j >= first_gate_tile)
    def _():
        o_ref[...] = jax.nn.sigmoid(acc).astype(o_ref.dtype)


def _inproj(h, w_all, dm):
    m, d = h.shape
    n = w_all.shape[1]
    tm = min(1024, m)
    tn = 1024
    assert n % tn == 0 and (dm.cb_gate * LANE) % tn == 0
    return pl.pallas_call(
        functools.partial(_inproj_kernel, first_gate_tile=dm.cb_gate * LANE // tn),
        grid=(m // tm, n // tn),
        in_specs=[pl.BlockSpec((tm, d), lambda i, j: (i, 0)),
                  pl.BlockSpec((d, tn), lambda i, j: (0, j))],
        out_specs=pl.BlockSpec((tm, tn), lambda i, j: (i, j)),
        out_shape=jax.ShapeDtypeStruct((m, n), BF16),
        compiler_params=_params("parallel", "parallel"),
        name="inproj",
    )(h, w_all)


def _fgate_kernel(h_ref, wf_ref, bf_ref, o_ref, carry_ref, *, tile):
    @pl.when(pl.program_id(1) == 0)
    def _():
        carry_ref[...] = jnp.zeros_like(carry_ref)

    f = jnp.dot(h_ref[...], wf_ref[...], preferred_element_type=F32) + bf_ref[...]
    ls = jnp.minimum(f, 0.0) - jnp.log1p(jnp.exp(-jnp.abs(f)))
    row = lax.broadcasted_iota(jnp.int32, (tile, tile), 0)
    col = lax.broadcasted_iota(jnp.int32, (tile, tile), 1)
    tri = jnp.where(row >= col, 1.0, 0.0).astype(BF16)
    hi = ls.astype(BF16)
    r1 = ls - hi.astype(F32)
    mid = r1.astype(BF16)
    lo = (r1 - mid.astype(F32)).astype(BF16)
    c = (jnp.dot(tri, hi, preferred_element_type=F32)
         + jnp.dot(tri, mid, preferred_element_type=F32)
         + jnp.dot(tri, lo, preferred_element_type=F32)) + carry_ref[...]
    carry_ref[...] = c[tile - 1:tile, :]
    o_ref[0] = c.T[0:8, :]


def _fgate(h, wf, bf, dm):
    m, d = h.shape
    tile = 512
    nt = dm.seq // tile
    assert dm.fox_heads <= 8
    return pl.pallas_call(
        functools.partial(_fgate_kernel, tile=tile),
        grid=(dm.batch, nt),
        in_specs=[pl.BlockSpec((tile, d), lambda b, t: (b * nt + t, 0)),
                  pl.BlockSpec((d, LANE), lambda b, t: (0, 0)),
                  pl.BlockSpec((1, LANE), lambda b, t: (0, 0))],
        out_specs=pl.BlockSpec((1, 8, tile), lambda b, t: (b, 0, t)),
        out_shape=jax.ShapeDtypeStruct((dm.batch, 8, dm.seq), F32),
        scratch_shapes=[pltpu.VMEM((1, LANE), F32)],
        compiler_params=_params("parallel", "arbitrary"),
        name="fgate",
    )(h, wf, bf)


def _dilated_kernel(q0, k0, v0, q1, k1, v1, q2, k2, v2, gate_ref, o_ref,
                    src, q_rm, k_rm, v_rm, o_rm, lse_rm,
                    o_t0, o_t1, o_t2, l_t0, l_t1, l_t2, *, seq):
    scale = HEAD_DIM ** -0.5
    qi = lax.broadcasted_iota(jnp.int32, (BAND, 2 * BAND), 0)
    kc = lax.broadcasted_iota(jnp.int32, (BAND, 2 * BAND), 1)
    upper = qi + BAND
    groups = ((q0, k0, v0, o_t0, l_t0), (q1, k1, v1, o_t1, l_t1), (q2, k2, v2, o_t2, l_t2))

    for (q_ref, k_ref, v_ref, o_tok, l_tok), (_, dil) in zip(groups, DILATED_CFG):
        sub = seq // dil
        nb = sub // BAND
        padded = sub + BAND

        def to_residue_major(ref, dst, pad):
            if dil > 1:
                src[...] = ref[...].astype(F32)
            for r in range(dil):
                if dil > 1:
                    rows = src[pl.ds(r, sub, stride=dil), :].astype(BF16)
                else:
                    rows = ref[...]
                base = r * (padded if pad else sub) + (BAND if pad else 0)
                dst[base:base + sub, :] = rows
                if pad:
                    dst[r * padded:r * padded + BAND, :] = jnp.zeros((BAND, HEAD_DIM), BF16)

        to_residue_major(q_ref, q_rm, False)
        to_residue_major(k_ref, k_rm, True)
        to_residue_major(v_ref, v_rm, True)
        o_dst, l_dst = (o_tok, l_tok) if dil == 1 else (o_rm, lse_rm)

        def block(t, carry):
            r = t // nb
            n = t - r * nb
            qoff = pl.multiple_of(t * BAND, BAND)
            koff = pl.multiple_of((t + r) * BAND, BAND)
            qb = q_rm[pl.ds(qoff, BAND), :]
            kb = k_rm[pl.ds(koff, 2 * BAND), :]
            vb = v_rm[pl.ds(koff, 2 * BAND), :]
            s = lax.dot_general(qb, kb, (((1,), (1,)), ((), ())),
                                preferred_element_type=F32) * scale
            first_key = jnp.where(n > 0, 0, BAND)
            valid = (kc >= jnp.maximum(qi, first_key)) & (kc <= upper)
            s = jnp.where(valid, s, -jnp.inf)
            m = jnp.max(s, axis=-1, keepdims=True)
            p = jnp.exp(s - m)
            l = jnp.sum(p, axis=-1, keepdims=True)
            o = jnp.dot(p.astype(BF16), vb, preferred_element_type=F32) / l
            o_dst[pl.ds(qoff, BAND), :] = o
            l_dst[pl.ds(qoff, BAND), :] = jnp.broadcast_to(m + jnp.log(l), (BAND, HEAD_DIM))
            return carry

        lax.fori_loop(0, dil * nb, block, 0)

        if dil > 1:
            for r in range(dil):
                o_tok[pl.ds(r, sub, stride=dil), :] = o_rm[r * sub:(r + 1) * sub, :]
                l_tok[pl.ds(r, sub, stride=dil), :] = lse_rm[r * sub:(r + 1) * sub, :]

    l0, l1, l2 = l_t0[...], l_t1[...], l_t2[...]
    mx = jnp.maximum(jnp.maximum(l0, l1), l2)
    e0, e1, e2 = jnp.exp(l0 - mx), jnp.exp(l1 - mx), jnp.exp(l2 - mx)
    o = (e0 * o_t0[...] + e1 * o_t1[...] + e2 * o_t2[...]) / (e0 + e1 + e2)
    o_ref[...] = (o * _silu(gate_ref[...].astype(F32))).astype(o_ref.dtype)


def _dilated(pg, dm):
    seq = dm.seq
    max_dil = max(d for _, d in DILATED_CFG)
    assert seq % (BAND * max_dil) == 0

    def col_spec(cb, g):
        return pl.BlockSpec((seq, HEAD_DIM), lambda b, j, cb=cb, g=g: (b, cb + g * dm.hpg + j))

    in_specs = []
    for g in range(len(DILATED_CFG)):
        in_specs += [col_spec(dm.cb_aq, g), col_spec(dm.cb_ak, g), col_spec(dm.cb_av, g)]
    in_specs.append(pl.BlockSpec((seq, HEAD_DIM), lambda b, j: (b, dm.cb_ag + j)))
    tok = pltpu.VMEM((seq, HEAD_DIM), F32)
    return pl.pallas_call(
        functools.partial(_dilated_kernel, seq=seq),
        grid=(dm.batch, dm.hpg),
        in_specs=in_specs,
        out_specs=pl.BlockSpec((seq, HEAD_DIM), lambda b, j: (b, j)),
        out_shape=jax.ShapeDtypeStruct((dm.batch * seq, dm.hpg * HEAD_DIM), BF16),
        scratch_shapes=[tok,
                        pltpu.VMEM((seq, HEAD_DIM), BF16),
                        pltpu.VMEM((seq + max_dil * BAND, HEAD_DIM), BF16),
                        pltpu.VMEM((seq + max_dil * BAND, HEAD_DIM), BF16),
                        tok, tok, tok, tok, tok, tok, tok, tok],
        compiler_params=_params("parallel", "parallel"),
        name="dilated_attn",
    )(*([pg] * 10))


def _conv_kernel(a_ref, b_ref, ah_ref, bh_ref, gate_ref, cw_ref, cb_ref, lg_ref, lb_ref,
                 pw_ref, pwb_ref, o_ref, ybuf, cbuf, *, tile, rows):
    width = ybuf.shape[1]
    first = pl.program_id(1) == 0
    ybuf[CONV_HALO:CONV_HALO + tile, :] = (
        a_ref[...].astype(F32) * jax.nn.sigmoid(b_ref[...].astype(F32)))
    halo = ah_ref[...].astype(F32) * jax.nn.sigmoid(bh_ref[...].astype(F32))
    ybuf[0:CONV_HALO, :] = jnp.where(first, 0.0, halo)

    shift = CONV_HALO - (CONV_K - 1)
    for c in range(width // LANE):
        cs = slice(c * LANE, (c + 1) * LANE)
        taps = [cw_ref[j:j + 1, cs] for j in range(CONV_K)]
        bias = cb_ref[:, cs]
        for rc in range(tile // rows):
            acc = jnp.zeros((rows, LANE), F32)
            for j in range(CONV_K):
                start = rc * rows + shift + j
                acc = acc + taps[j] * ybuf[start:start + rows, cs]
            cbuf[rc * rows:(rc + 1) * rows, cs] = acc + bias

    y = cbuf[...]
    mu = jnp.mean(y, axis=-1, keepdims=True)
    yc = y - mu
    var = jnp.mean(yc * yc, axis=-1, keepdims=True)
    yn = yc * lax.rsqrt(var + EPS) * lg_ref[...] + lb_ref[...]
    z = _silu(yn).astype(BF16)
    out = jnp.dot(z, pw_ref[...], preferred_element_type=F32) + pwb_ref[...]
    o_ref[...] = (out * _silu(gate_ref[...].astype(F32))).astype(o_ref.dtype)


def _conv(pg, cw, cb, lg, lb, pw, pwb, dm):
    width = dm.w4 * LANE
    tile = 512
    nt = dm.seq // tile
    hb = tile // CONV_HALO

    def halo_rows(b, i):
        return jnp.maximum((b * nt + i) * hb - 1, 0)

    ca, cbb, cg = dm.cb_ba // dm.w4, dm.cb_bb // dm.w4, dm.cb_bg // dm.w4
    vec = pl.BlockSpec((1, width), lambda b, i: (0, 0))
    return pl.pallas_call(
        functools.partial(_conv_kernel, tile=tile, rows=128),
        grid=(dm.batch, nt),
        in_specs=[pl.BlockSpec((tile, width), lambda b, i: (b * nt + i, ca)),
                  pl.BlockSpec((tile, width), lambda b, i: (b * nt + i, cbb)),
                  pl.BlockSpec((CONV_HALO, width), lambda b, i: (halo_rows(b, i), ca)),
                  pl.BlockSpec((CONV_HALO, width), lambda b, i: (halo_rows(b, i), cbb)),
                  pl.BlockSpec((tile, width), lambda b, i: (b * nt + i, cg)),
                  pl.BlockSpec((CONV_K, width), lambda b, i: (0, 0)),
                  vec, vec, vec,
                  pl.BlockSpec((width, width), lambda b, i: (0, 0)),
                  vec],
        out_specs=pl.BlockSpec((tile, width), lambda b, i: (b * nt + i, 0)),
        out_shape=jax.ShapeDtypeStruct((dm.batch * dm.seq, width), BF16),
        scratch_shapes=[pltpu.VMEM((CONV_HALO + tile, width), F32),
                        pltpu.VMEM((tile, width), F32)],
        compiler_params=_params("parallel", "parallel"),
        name="conformer_conv",
    )(pg, pg, pg, pg, pg, cw, cb.reshape(1, width), lg.reshape(1, width), lb.reshape(1, width),
      pw, pwb.reshape(1, width))


def _fox_kernel(q_ref, k_ref, v_ref, gate_ref, c_ref, o_ref, *, tq):
    i = pl.program_id(2)
    q = q_ref[...]
    scale = HEAD_DIM ** -0.5

    def scores(j):
        off = pl.multiple_of(j * tq, tq)
        kb = k_ref[pl.ds(off, tq), :]
        s = lax.dot_general(q, kb, (((1,), (1,)), ((), ())), preferred_element_type=F32) * scale
        return s - c_ref[0, :, pl.ds(off, tq)], off

    def update(s, off, carry):
        m, l, acc = carry
        m_new = jnp.maximum(m, jnp.max(s, axis=-1, keepdims=True))
        alpha = jnp.exp(m - m_new)
        p = jnp.exp(s - m_new)
        l = alpha * l + jnp.sum(p, axis=-1, keepdims=True)
        vb = v_ref[pl.ds(off, tq), :]
        acc = alpha * acc + jnp.dot(p.astype(BF16), vb, preferred_element_type=F32)
        return m_new, l, acc

    def below_diagonal(j, carry):
        s, off = scores(j)
        return update(s, off, carry)

    init = (jnp.full((tq, 1), -jnp.inf, F32), jnp.zeros((tq, 1), F32),
            jnp.zeros((tq, HEAD_DIM), F32))
    carry = lax.fori_loop(0, i, below_diagonal, init)
    s, off = scores(i)
    row = lax.broadcasted_iota(jnp.int32, (tq, tq), 0)
    col = lax.broadcasted_iota(jnp.int32, (tq, tq), 1)
    _, l, acc = update(jnp.where(col <= row, s, -jnp.inf), off, carry)
    o_ref[...] = (acc / l * _silu(gate_ref[...].astype(F32))).astype(o_ref.dtype)


def _fox(pg, c_rows, dm):
    seq, heads = dm.seq, dm.fox_heads
    tq = 512
    nq = seq // tq
    return pl.pallas_call(
        functools.partial(_fox_kernel, tq=tq),
        grid=(dm.batch, heads, nq),
        in_specs=[pl.BlockSpec((tq, HEAD_DIM), lambda b, h, i: (b * nq + i, dm.cb_cq + h)),
                  pl.BlockSpec((seq, HEAD_DIM), lambda b, h, i: (b, dm.cb_ck + h)),
                  pl.BlockSpec((seq, HEAD_DIM), lambda b, h, i: (b, dm.cb_cv + h)),
                  pl.BlockSpec((tq, HEAD_DIM), lambda b, h, i: (b * nq + i, dm.cb_cg + h)),
                  pl.BlockSpec((1, 1, seq), lambda b, h, i: (b * 8 + h, 0, 0))],
        out_specs=pl.BlockSpec((tq, HEAD_DIM), lambda b, h, i: (b * nq + i, h)),
        out_shape=jax.ShapeDtypeStruct((dm.batch * seq, heads * HEAD_DIM), BF16),
        compiler_params=_params("parallel", "parallel", "parallel"),
        name="fox_attn",
    )(pg, pg, pg, pg, c_rows)


def _pool_kernel(x_ref, xh_ref, gate_ref, pw_ref, ps_ref, o_ref, xbuf, *, tile):
    i = pl.program_id(1)
    xbuf[POOL_HALO:POOL_HALO + tile, :] = x_ref[...].astype(F32)
    xbuf[0:POOL_HALO, :] = jnp.where(i == 0, 0.0, xh_ref[...].astype(F32))
    gc = xbuf.shape[1] // len(POOL_SIZES)
    t = i * tile + lax.broadcasted_iota(jnp.int32, (tile, 1), 0)
    for g, p in enumerate(POOL_SIZES):
        cs = slice(g * gc, (g + 1) * gc)
        x = xbuf[POOL_HALO:POOL_HALO + tile, cs]
        win = x
        for k in range(1, p):
            win = win + xbuf[POOL_HALO - k:POOL_HALO - k + tile, cs]
        cnt = jnp.minimum(t + 1, p).astype(F32)
        d = (win / cnt - x).astype(BF16)
        y = jnp.dot(d, pw_ref[g], preferred_element_type=F32) * ps_ref[:, cs]
        o_ref[:, cs] = (y * _silu(gate_ref[:, cs].astype(F32))).astype(o_ref.dtype)


def _pool(pg, pw, ps, dm):
    width = dm.w4 * LANE
    tile = 512
    nt = dm.seq // tile
    hb = tile // POOL_HALO
    gc = width // len(POOL_SIZES)
    ci, cg = dm.cb_di // dm.w4, dm.cb_dg // dm.w4
    return pl.pallas_call(
        functools.partial(_pool_kernel, tile=tile),
        grid=(dm.batch, nt),
        in_specs=[pl.BlockSpec((tile, width), lambda b, i: (b * nt + i, ci)),
                  pl.BlockSpec((POOL_HALO, width),
                               lambda b, i: (jnp.maximum((b * nt + i) * hb - 1, 0), ci)),
                  pl.BlockSpec((tile, width), lambda b, i: (b * nt + i, cg)),
                  pl.BlockSpec((len(POOL_SIZES), gc, gc), lambda b, i: (0, 0, 0)),
                  pl.BlockSpec((1, width), lambda b, i: (0, 0))],
        out_specs=pl.BlockSpec((tile, width), lambda b, i: (b * nt + i, 0)),
        out_shape=jax.ShapeDtypeStruct((dm.batch * dm.seq, width), BF16),
        scratch_shapes=[pltpu.VMEM((POOL_HALO + tile, width), F32)],
        compiler_params=_params("parallel", "parallel"),
        name="pool_mixer",
    )(pg, pg, pg, pw, ps.reshape(1, width))


def _merge_kernel(ga, gb, gc, gd, ya, yb, yc, yd, wa, wb, wc, wd, o_ref):
    acc = None
    for g, y, w in ((ga, ya, wa), (gb, yb, wb), (gc, yc, wc), (gd, yd, wd)):
        u = jnp.dot(y[...], w[...], preferred_element_type=F32)
        term = g[...].astype(F32) * u
        acc = term if acc is None else acc + term
    o_ref[...] = acc.astype(o_ref.dtype)


def _merge(pg, ys, ws, dm):
    m = dm.batch * dm.seq
    d = dm.d_model
    tm = min(1024, m)
    tn = 512
    gate0 = dm.cb_gate * LANE // tn
    per_gate = d // tn
    in_specs = [pl.BlockSpec((tm, tn), lambda i, j, k=k: (i, gate0 + k * per_gate + j))
                for k in range(4)]
    in_specs += [pl.BlockSpec((tm, y.shape[1]), lambda i, j: (i, 0)) for y in ys]
    in_specs += [pl.BlockSpec((w.shape[0], tn), lambda i, j: (0, j)) for w in ws]
    return pl.pallas_call(
        _merge_kernel,
        grid=(m // tm, d // tn),
        in_specs=in_specs,
        out_specs=pl.BlockSpec((tm, tn), lambda i, j: (i, j)),
        out_shape=jax.ShapeDtypeStruct((m, d), BF16),
        compiler_params=_params("parallel", "parallel"),
        name="gated_merge",
    )(pg, pg, pg, pg, *ys, *ws)


def _outproj_kernel(m_ref, w_ref, x_ref, o_ref):
    o_ref[...] = x_ref[...] + jnp.dot(m_ref[...], w_ref[...], preferred_element_type=F32)


def _outproj(merged, w_out, x2):
    m, d = x2.shape
    tm = min(1024, m)
    tn = 512
    return pl.pallas_call(
        _outproj_kernel,
        grid=(m // tm, d // tn),
        in_specs=[pl.BlockSpec((tm, d), lambda i, j: (i, 0)),
                  pl.BlockSpec((d, tn), lambda i, j: (0, j)),
                  pl.BlockSpec((tm, tn), lambda i, j: (i, j))],
        out_specs=pl.BlockSpec((tm, tn), lambda i, j: (i, j)),
        out_shape=jax.ShapeDtypeStruct((m, d), F32),
        compiler_params=_params("parallel", "parallel"),
        name="outproj",
    )(merged, w_out, x2)


def kernel(x, norm_g, w_in, b_forget, conv_w, conv_b, conv_ln_g, conv_ln_b, conv_pw, conv_pw_b,
           pool_w, pool_scale, w_branch, w_out, final_g):
    dm = _dims(x.shape)
    depth = norm_g.shape[0]
    x2 = x.reshape(dm.batch * dm.seq, dm.d_model)
    heads = dm.fox_heads
    row_off = (0, dm.hpg * LANE, (dm.hpg + dm.w4) * LANE, (dm.hpg + 2 * dm.w4) * LANE,
               (dm.hpg + 3 * dm.w4) * LANE)

    for l in range(depth):
        wi = w_in[l]
        w_all = jnp.concatenate([wi[:, :dm.cf_col], wi[:, dm.cf_col + heads:]], axis=1).astype(BF16)
        wf = jnp.pad(wi[:, dm.cf_col:dm.cf_col + heads], ((0, 0), (0, LANE - heads))).astype(BF16)
        bf = jnp.pad(b_forget[l], (0, LANE - heads)).reshape(1, LANE)
        wb = w_branch[l].astype(BF16)
        ws = [wb[row_off[k]:row_off[k + 1]] for k in range(4)]

        h = _rmsnorm(x2, norm_g[l], BF16)
        pg = _inproj(h, w_all, dm)
        c_rows = _fgate(h, wf, bf, dm).reshape(dm.batch * 8, 1, dm.seq)
        y_a = _dilated(pg, dm)
        y_b = _conv(pg, conv_w[l], conv_b[l], conv_ln_g[l], conv_ln_b[l],
                    conv_pw[l].astype(BF16), conv_pw_b[l], dm)
        y_c = _fox(pg, c_rows, dm)
        y_d = _pool(pg, pool_w[l].astype(BF16), pool_scale[l], dm)
        merged = _merge(pg, (y_a, y_b, y_c, y_d), ws, dm)
        x2 = _outproj(merged, w_out[l].astype(BF16), x2)

    return _rmsnorm(x2, final_g, x.dtype).reshape(x.shape)
```

```python
import functools
import math
from typing import NamedTuple

import jax
import jax.numpy as jnp
from jax import lax
from jax.experimental import pallas as pl
from jax.experimental.pallas import tpu as pltpu

F32 = jnp.float32
BF16 = jnp.bfloat16

EPS = 1e-6
HEAD_DIM = 128
LANE = 128
DILATED_CFG = ((128, 1), (512, 4), (2048, 16))
BAND = 128
CONV_K = 31
CONV_HALO = 32
POOL_SIZES = (2, 4, 8, 16)
POOL_HALO = 16
VMEM_LIMIT = 56 * 1024 * 1024
LOG2E = math.log2(math.e)


class _Dims(NamedTuple):
    batch: int
    seq: int
    d_model: int
    hpg: int
    w4: int
    fox_heads: int
    lo_aq: int
    lo_ak: int
    lo_av: int
    lo_ag: int
    lo_ba: int
    lo_bb: int
    lo_bg: int
    lo_cq: int
    lo_ck: int
    lo_cv: int
    lo_blocks: int
    hi_cg: int
    hi_di: int
    hi_dg: int
    hi_gate: int
    hi_blocks: int


def _dims(x_shape):
    batch, seq, d = x_shape
    hpg = d // 1024
    w4 = d // 512
    lo_ba = 10 * hpg
    lo_cq = lo_ba + 3 * w4
    return _Dims(batch=batch, seq=seq, d_model=d, hpg=hpg, w4=w4, fox_heads=d // 512,
                 lo_aq=0, lo_ak=3 * hpg, lo_av=6 * hpg, lo_ag=9 * hpg,
                 lo_ba=lo_ba, lo_bb=lo_ba + w4, lo_bg=lo_ba + 2 * w4,
                 lo_cq=lo_cq, lo_ck=lo_cq + w4, lo_cv=lo_cq + 2 * w4, lo_blocks=lo_cq + 3 * w4,
                 hi_cg=0, hi_di=w4, hi_dg=2 * w4, hi_gate=3 * w4,
                 hi_blocks=3 * w4 + 4 * (d // LANE))


def _params(*sem):
    return pltpu.CompilerParams(dimension_semantics=sem, vmem_limit_bytes=VMEM_LIMIT)


def _silu(x):
    return x * jax.nn.sigmoid(x)


def _rmsnorm_kernel(x_ref, g_ref, o_ref):
    x = x_ref[...]
    ms = jnp.mean(x * x, axis=-1, keepdims=True)
    o_ref[...] = (x * lax.rsqrt(ms + EPS) * g_ref[...]).astype(o_ref.dtype)


def _rmsnorm(x2, g, out_dtype):
    m, d = x2.shape
    tm = min(512, m)
    return pl.pallas_call(
        _rmsnorm_kernel,
        grid=(m // tm,),
        in_specs=[pl.BlockSpec((tm, d), lambda i: (i, 0)),
                  pl.BlockSpec((1, d), lambda i: (0, 0))],
        out_specs=pl.BlockSpec((tm, d), lambda i: (i, 0)),
        out_shape=jax.ShapeDtypeStruct((m, d), out_dtype),
        compiler_params=_params("parallel"),
        name="rmsnorm",
    )(x2, g.reshape(1, d))


def _inproj_kernel(h_ref, w_ref, o_ref, *, first_gate_tile):
    acc = jnp.dot(h_ref[...], w_ref[...], preferred_element_type=F32)
    j = pl.program_id(1)

    @pl.when(j < first_gate_tile)
    def _():
        o_ref[...] = acc.astype(o_ref.dtype)

    @pl.when(j >= first_gate_tile)
    def _():
        o_ref[...] = jax.nn.sigmoid(acc).astype(o_ref.dtype)


def _inproj(h, w, first_gate_block):
    m, d = h.shape
    n = w.shape[1]
    tm = min(1024, m)
    tn = 1024 if (n % 1024 == 0 and (first_gate_block * LANE) % 1024 == 0) else 512
    assert n % tn == 0 and (first_gate_block * LANE) % tn == 0
    return pl.pallas_call(
        functools.partial(_inproj_kernel, first_gate_tile=first_gate_block * LANE // tn),
        grid=(m // tm, n // tn),
        in_specs=[pl.BlockSpec((tm, d), lambda i, j: (i, 0)),
                  pl.BlockSpec((d, tn), lambda i, j: (0, j))],
        out_specs=pl.BlockSpec((tm, tn), lambda i, j: (i, j)),
        out_shape=jax.ShapeDtypeStruct((m, n), BF16),
        compiler_params=_params("parallel", "parallel"),
        name="inproj",
    )(h, w)


def _fgate_kernel(h_ref, wf_ref, bf_ref, o_ref, carry_ref, *, tile):
    @pl.when(pl.program_id(1) == 0)
    def _():
        carry_ref[...] = jnp.zeros_like(carry_ref)

    f = jnp.dot(h_ref[...], wf_ref[...], preferred_element_type=F32) + bf_ref[...]
    ls = jnp.minimum(f, 0.0) - jnp.log1p(jnp.exp(-jnp.abs(f)))
    row = lax.broadcasted_iota(jnp.int32, (tile, tile), 0)
    col = lax.broadcasted_iota(jnp.int32, (tile, tile), 1)
    tri = jnp.where(row >= col, 1.0, 0.0).astype(BF16)
    hi = ls.astype(BF16)
    r1 = ls - hi.astype(F32)
    mid = r1.astype(BF16)
    lo = (r1 - mid.astype(F32)).astype(BF16)
    c = (jnp.dot(tri, hi, preferred_element_type=F32)
         + jnp.dot(tri, mid, preferred_element_type=F32)
         + jnp.dot(tri, lo, preferred_element_type=F32)) + carry_ref[...]
    carry_ref[...] = c[tile - 1:tile, :]
    o_ref[0] = c.T[0:8, :]


def _fgate(h, wf, bf, dm):
    m, d = h.shape
    tile = 512
    nt = dm.seq // tile
    assert dm.fox_heads <= 8
    return pl.pallas_call(
        functools.partial(_fgate_kernel, tile=tile),
        grid=(dm.batch, nt),
        in_specs=[pl.BlockSpec((tile, d), lambda b, t: (b * nt + t, 0)),
                  pl.BlockSpec((d, LANE), lambda b, t: (0, 0)),
                  pl.BlockSpec((1, LANE), lambda b, t: (0, 0))],
        out_specs=pl.BlockSpec((1, 8, tile), lambda b, t: (b, 0, t)),
        out_shape=jax.ShapeDtypeStruct((dm.batch, 8, dm.seq), F32),
        scratch_shapes=[pltpu.VMEM((1, LANE), F32)],
        compiler_params=_params("parallel", "arbitrary"),
        name="fgate",
    )(h, wf, bf)


def _dilated_group_kernel(q_ref, k_ref, v_ref, o_ref, l_ref, *, nb, hpg):
    scale = HEAD_DIM ** -0.5

    def attend(qb, kb, vb, valid):
        s = lax.dot_general(qb, kb, (((1,), (1,)), ((), ())), preferred_element_type=F32) * scale
        s = jnp.where(valid, s, -jnp.inf)
        m = jnp.max(s, axis=-1, keepdims=True)
        p = jnp.exp(s - m)
        l = jnp.sum(p, axis=-1, keepdims=True)
        o = jnp.dot(p.astype(BF16), vb, preferred_element_type=F32) / l
        return o, jnp.broadcast_to(m + jnp.log(l), (BAND, HEAD_DIM))

    qi = lax.broadcasted_iota(jnp.int32, (BAND, BAND), 0)
    kc = lax.broadcasted_iota(jnp.int32, (BAND, BAND), 1)
    causal = kc <= qi
    for h in range(hpg):
        cs = slice(h * HEAD_DIM, (h + 1) * HEAD_DIM)
        o, lse = attend(q_ref[0:BAND, cs], k_ref[0:BAND, cs], v_ref[0:BAND, cs], causal)
        o_ref[0:BAND, cs] = o
        l_ref[0:BAND, cs] = lse

    if nb > 1:
        qi2 = lax.broadcasted_iota(jnp.int32, (BAND, 2 * BAND), 0)
        kc2 = lax.broadcasted_iota(jnp.int32, (BAND, 2 * BAND), 1)
        band = (kc2 >= qi2) & (kc2 <= qi2 + BAND)

        def block(n, carry):
            qoff = pl.multiple_of(n * BAND, BAND)
            koff = pl.multiple_of((n - 1) * BAND, BAND)
            for h in range(hpg):
                cs = slice(h * HEAD_DIM, (h + 1) * HEAD_DIM)
                o, lse = attend(q_ref[pl.ds(qoff, BAND), cs], k_ref[pl.ds(koff, 2 * BAND), cs],
                                v_ref[pl.ds(koff, 2 * BAND), cs], band)
                o_ref[pl.ds(qoff, BAND), cs] = o
                l_ref[pl.ds(qoff, BAND), cs] = lse
            return carry

        lax.fori_loop(1, nb, block, 0)


def _dilated_group(pg_lo, g, dil, dm):
    sub = dm.seq // dil
    nb = sub // BAND
    assert sub % BAND == 0
    width = dm.hpg * HEAD_DIM
    n_lo = dm.lo_blocks * LANE
    rows = dm.batch * sub
    view = pg_lo.reshape(rows, dil * n_lo)
    per_res = dm.lo_blocks // dm.hpg

    def col_spec(cb):
        first = (cb + g * dm.hpg) // dm.hpg
        return pl.BlockSpec((sub, width), lambda b, r: (b, r * per_res + first))

    out_spec = pl.BlockSpec((sub, width), lambda b, r: (b, r))
    out_shape = jax.ShapeDtypeStruct((rows, dil * width), F32)
    o, lse = pl.pallas_call(
        functools.partial(_dilated_group_kernel, nb=nb, hpg=dm.hpg),
        grid=(dm.batch, dil),
        in_specs=[col_spec(dm.lo_aq), col_spec(dm.lo_ak), col_spec(dm.lo_av)],
        out_specs=[out_spec, out_spec],
        out_shape=[out_shape, out_shape],
        compiler_params=_params("parallel", "parallel"),
        name=f"dilated_attn_d{dil}",
    )(view, view, view)
    tokens = dm.batch * dm.seq
    return o.reshape(tokens, width), lse.reshape(tokens, width)


def _dilated_mix_kernel(o0, o1, o2, l0, l1, l2, gate_ref, y_ref):
    a0, a1, a2 = l0[...], l1[...], l2[...]
    mx = jnp.maximum(jnp.maximum(a0, a1), a2)
    e0, e1, e2 = jnp.exp(a0 - mx), jnp.exp(a1 - mx), jnp.exp(a2 - mx)
    o = (e0 * o0[...] + e1 * o1[...] + e2 * o2[...]) / (e0 + e1 + e2)
    y_ref[...] = (o * _silu(gate_ref[...].astype(F32))).astype(y_ref.dtype)


def _dilated(pg_lo, dm):
    outs, lses = [], []
    for g, (_, dil) in enumerate(DILATED_CFG):
        o, lse = _dilated_group(pg_lo, g, dil, dm)
        outs.append(o)
        lses.append(lse)
    tokens = dm.batch * dm.seq
    width = dm.hpg * HEAD_DIM
    tm = min(1024, tokens)
    blk = pl.BlockSpec((tm, width), lambda i: (i, 0))
    return pl.pallas_call(
        _dilated_mix_kernel,
        grid=(tokens // tm,),
        in_specs=[blk] * 6 + [pl.BlockSpec((tm, width), lambda i: (i, dm.lo_ag // dm.hpg))],
        out_specs=blk,
        out_shape=jax.ShapeDtypeStruct((tokens, width), BF16),
        compiler_params=_params("parallel"),
        name="dilated_mix",
    )(*outs, *lses, pg_lo)


def _conv_kernel(a_ref, b_ref, ah_ref, bh_ref, gate_ref, cw_ref, cb_ref, lg_ref, lb_ref,
                 pw_ref, pwb_ref, o_ref, ybuf, cbuf, *, tile, rows):
    width = ybuf.shape[1]
    first = pl.program_id(1) == 0
    ybuf[CONV_HALO:CONV_HALO + tile, :] = (
        a_ref[...].astype(F32) * jax.nn.sigmoid(b_ref[...].astype(F32)))
    halo = ah_ref[...].astype(F32) * jax.nn.sigmoid(bh_ref[...].astype(F32))
    ybuf[0:CONV_HALO, :] = jnp.where(first, 0.0, halo)

    shift = CONV_HALO - (CONV_K - 1)
    for c in range(width // LANE):
        cs = slice(c * LANE, (c + 1) * LANE)
        taps = [cw_ref[j:j + 1, cs] for j in range(CONV_K)]
        bias = cb_ref[:, cs]
        for rc in range(tile // rows):
            acc = jnp.zeros((rows, LANE), F32)
            for j in range(CONV_K):
                start = rc * rows + shift + j
                acc = acc + taps[j] * ybuf[start:start + rows, cs]
            cbuf[rc * rows:(rc + 1) * rows, cs] = acc + bias

    y = cbuf[...]
    mu = jnp.mean(y, axis=-1, keepdims=True)
    yc = y - mu
    var = jnp.mean(yc * yc, axis=-1, keepdims=True)
    yn = yc * lax.rsqrt(var + EPS) * lg_ref[...] + lb_ref[...]
    z = _silu(yn).astype(BF16)
    out = jnp.dot(z, pw_ref[...], preferred_element_type=F32) + pwb_ref[...]
    o_ref[...] = (out * _silu(gate_ref[...].astype(F32))).astype(o_ref.dtype)


def _conv(pg_lo, cw, cb, lg, lb, pw, pwb, dm):
    width = dm.w4 * LANE
    tile = 512
    nt = dm.seq // tile
    hb = tile // CONV_HALO

    def halo_rows(b, i):
        return jnp.maximum((b * nt + i) * hb - 1, 0)

    ca, cbb, cg = dm.lo_ba // dm.w4, dm.lo_bb // dm.w4, dm.lo_bg // dm.w4
    vec = pl.BlockSpec((1, width), lambda b, i: (0, 0))
    return pl.pallas_call(
        functools.partial(_conv_kernel, tile=tile, rows=128),
        grid=(dm.batch, nt),
        in_specs=[pl.BlockSpec((tile, width), lambda b, i: (b * nt + i, ca)),
                  pl.BlockSpec((tile, width), lambda b, i: (b * nt + i, cbb)),
                  pl.BlockSpec((CONV_HALO, width), lambda b, i: (halo_rows(b, i), ca)),
                  pl.BlockSpec((CONV_HALO, width), lambda b, i: (halo_rows(b, i), cbb)),
                  pl.BlockSpec((tile, width), lambda b, i: (b * nt + i, cg)),
                  pl.BlockSpec((CONV_K, width), lambda b, i: (0, 0)),
                  vec, vec, vec,
                  pl.BlockSpec((width, width), lambda b, i: (0, 0)),
                  vec],
        out_specs=pl.BlockSpec((tile, width), lambda b, i: (b * nt + i, 0)),
        out_shape=jax.ShapeDtypeStruct((dm.batch * dm.seq, width), BF16),
        scratch_shapes=[pltpu.VMEM((CONV_HALO + tile, width), F32),
                        pltpu.VMEM((tile, width), F32)],
        compiler_params=_params("parallel", "parallel"),
        name="conformer_conv",
    )(pg_lo, pg_lo, pg_lo, pg_lo, pg_lo, cw, cb.reshape(1, width), lg.reshape(1, width),
      lb.reshape(1, width), pw, pwb.reshape(1, width))


def _fox_kernel(q_ref, k_ref, v_ref, gate_ref, c_ref, o_ref, *, tq):
    i = pl.program_id(2)
    q = q_ref[...]
    scale2 = HEAD_DIM ** -0.5 * LOG2E

    def scores(j):
        off = pl.multiple_of(j * tq, tq)
        kb = k_ref[pl.ds(off, tq), :]
        s = lax.dot_general(q, kb, (((1,), (1,)), ((), ())), preferred_element_type=F32)
        return s * scale2 + c_ref[0, :, pl.ds(off, tq)] * (-LOG2E), off

    def update(s, off, carry):
        m, l, acc = carry
        m_new = jnp.maximum(m, jnp.max(s, axis=-1, keepdims=True))
        alpha = jnp.exp2(m - m_new)
        p = jnp.exp2(s - m_new)
        l = alpha * l + jnp.sum(p, axis=-1, keepdims=True)
        vb = v_ref[pl.ds(off, tq), :]
        acc = alpha * acc + jnp.dot(p.astype(BF16), vb, preferred_element_type=F32)
        return m_new, l, acc

    def below_diagonal(j, carry):
        s, off = scores(j)
        return update(s, off, carry)

    init = (jnp.full((tq, 1), -jnp.inf, F32), jnp.zeros((tq, 1), F32),
            jnp.zeros((tq, HEAD_DIM), F32))
    carry = lax.fori_loop(0, i, below_diagonal, init)
    s, off = scores(i)
    row = lax.broadcasted_iota(jnp.int32, (tq, tq), 0)
    col = lax.broadcasted_iota(jnp.int32, (tq, tq), 1)
    _, l, acc = update(jnp.where(col <= row, s, -jnp.inf), off, carry)
    o_ref[...] = (acc / l * _silu(gate_ref[...].astype(F32))).astype(o_ref.dtype)


def _fox(pg_lo, pg_hi, c_rows, dm):
    seq, heads = dm.seq, dm.fox_heads
    tq = 512
    nq = seq // tq
    return pl.pallas_call(
        functools.partial(_fox_kernel, tq=tq),
        grid=(dm.batch, heads, nq),
        in_specs=[pl.BlockSpec((tq, HEAD_DIM), lambda b, h, i: (b * nq + i, dm.lo_cq + h)),
                  pl.BlockSpec((seq, HEAD_DIM), lambda b, h, i: (b, dm.lo_ck + h)),
                  pl.BlockSpec((seq, HEAD_DIM), lambda b, h, i: (b, dm.lo_cv + h)),
                  pl.BlockSpec((tq, HEAD_DIM), lambda b, h, i: (b * nq + i, dm.hi_cg + h)),
                  pl.BlockSpec((1, 1, seq), lambda b, h, i: (b * 8 + h, 0, 0))],
        out_specs=pl.BlockSpec((tq, HEAD_DIM), lambda b, h, i: (b * nq + i, h)),
        out_shape=jax.ShapeDtypeStruct((dm.batch * seq, heads * HEAD_DIM), BF16),
        compiler_params=_params("parallel", "parallel", "parallel"),
        name="fox_attn",
    )(pg_lo, pg_lo, pg_lo, pg_hi, c_rows)


def _pool_kernel(x_ref, xh_ref, gate_ref, pw_ref, ps_ref, o_ref, xbuf, *, tile):
    i = pl.program_id(1)
    xbuf[POOL_HALO:POOL_HALO + tile, :] = x_ref[...].astype(F32)
    xbuf[0:POOL_HALO, :] = jnp.where(i == 0, 0.0, xh_ref[...].astype(F32))
    gc = xbuf.shape[1] // len(POOL_SIZES)
    t = i * tile + lax.broadcasted_iota(jnp.int32, (tile, 1), 0)
    for g, p in enumerate(POOL_SIZES):
        cs = slice(g * gc, (g + 1) * gc)
        x = xbuf[POOL_HALO:POOL_HALO + tile, cs]
        win = x
        for k in range(1, p):
            win = win + xbuf[POOL_HALO - k:POOL_HALO - k + tile, cs]
        cnt = jnp.minimum(t + 1, p).astype(F32)
        d = (win / cnt - x).astype(BF16)
        y = jnp.dot(d, pw_ref[g], preferred_element_type=F32) * ps_ref[:, cs]
        o_ref[:, cs] = (y * _silu(gate_ref[:, cs].astype(F32))).astype(o_ref.dtype)


def _pool(pg_hi, pw, ps, dm):
    width = dm.w4 * LANE
    tile = 512
    nt = dm.seq // tile
    hb = tile // POOL_HALO
    gc = width // len(POOL_SIZES)
    ci, cg = dm.hi_di // dm.w4, dm.hi_dg // dm.w4
    return pl.pallas_call(
        functools.partial(_pool_kernel, tile=tile),
        grid=(dm.batch, nt),
        in_specs=[pl.BlockSpec((tile, width), lambda b, i: (b * nt + i, ci)),
                  pl.BlockSpec((POOL_HALO, width),
                               lambda b, i: (jnp.maximum((b * nt + i) * hb - 1, 0), ci)),
                  pl.BlockSpec((tile, width), lambda b, i: (b * nt + i, cg)),
                  pl.BlockSpec((len(POOL_SIZES), gc, gc), lambda b, i: (0, 0, 0)),
                  pl.BlockSpec((1, width), lambda b, i: (0, 0))],
        out_specs=pl.BlockSpec((tile, width), lambda b, i: (b * nt + i, 0)),
        out_shape=jax.ShapeDtypeStruct((dm.batch * dm.seq, width), BF16),
        scratch_shapes=[pltpu.VMEM((POOL_HALO + tile, width), F32)],
        compiler_params=_params("parallel", "parallel"),
        name="pool_mixer",
    )(pg_hi, pg_hi, pg_hi, pw, ps.reshape(1, width))


def _merge_kernel(ga, gb, gc, gd, ya, yb, yc, yd, wa, wb, wc, wd, o_ref):
    acc = None
    for g, y, w in ((ga, ya, wa), (gb, yb, wb), (gc, yc, wc), (gd, yd, wd)):
        u = jnp.dot(y[...], w[...], preferred_element_type=F32)
        term = g[...].astype(F32) * u
        acc = term if acc is None else acc + term
    o_ref[...] = acc.astype(o_ref.dtype)


def _merge(pg_hi, ys, ws, dm):
    m = dm.batch * dm.seq
    d = dm.d_model
    tm = min(1024, m)
    tn = 512
    gate0 = dm.hi_gate * LANE // tn
    per_gate = d // tn
    in_specs = [pl.BlockSpec((tm, tn), lambda i, j, k=k: (i, gate0 + k * per_gate + j))
                for k in range(4)]
    in_specs += [pl.BlockSpec((tm, y.shape[1]), lambda i, j: (i, 0)) for y in ys]
    in_specs += [pl.BlockSpec((w.shape[0], tn), lambda i, j: (0, j)) for w in ws]
    return pl.pallas_call(
        _merge_kernel,
        grid=(m // tm, d // tn),
        in_specs=in_specs,
        out_specs=pl.BlockSpec((tm, tn), lambda i, j: (i, j)),
        out_shape=jax.ShapeDtypeStruct((m, d), BF16),
        compiler_params=_params("parallel", "parallel"),
        name="gated_merge",
    )(pg_hi, pg_hi, pg_hi, pg_hi, *ys, *ws)


def _outproj_kernel(m_ref, w_ref, x_ref, o_ref):
    o_ref[...] = x_ref[...] + jnp.dot(m_ref[...], w_ref[...], preferred_element_type=F32)


def _outproj(merged, w_out, x2):
    m, d = x2.shape
    tm = min(1024, m)
    tn = 512
    return pl.pallas_call(
        _outproj_kernel,
        grid=(m // tm, d // tn),
        in_specs=[pl.BlockSpec((tm, d), lambda i, j: (i, 0)),
                  pl.BlockSpec((d, tn), lambda i, j: (0, j)),
                  pl.BlockSpec((tm, tn), lambda i, j: (i, j))],
        out_specs=pl.BlockSpec((tm, tn), lambda i, j: (i, j)),
        out_shape=jax.ShapeDtypeStruct((m, d), F32),
        compiler_params=_params("parallel", "parallel"),
        name="outproj",
    )(merged, w_out, x2)


def kernel(x, norm_g, w_in, b_forget, conv_w, conv_b, conv_ln_g, conv_ln_b, conv_pw, conv_pw_b,
           pool_w, pool_scale, w_branch, w_out, final_g):
    dm = _dims(x.shape)
    depth = norm_g.shape[0]
    x2 = x.reshape(dm.batch * dm.seq, dm.d_model)
    heads = dm.fox_heads
    cf_col = dm.lo_blocks * LANE
    row_off = (0, dm.hpg * LANE, (dm.hpg + dm.w4) * LANE, (dm.hpg + 2 * dm.w4) * LANE,
               (dm.hpg + 3 * dm.w4) * LANE)

    for l in range(depth):
        wi = w_in[l]
        w_lo = wi[:, :cf_col].astype(BF16)
        w_hi = wi[:, cf_col + heads:].astype(BF16)
        wf = jnp.pad(wi[:, cf_col:cf_col + heads], ((0, 0), (0, LANE - heads))).astype(BF16)
        bf = jnp.pad(b_forget[l], (0, LANE - heads)).reshape(1, LANE)
        ws = [w_branch[l, row_off[k]:row_off[k + 1]].astype(BF16) for k in range(4)]

        h = _rmsnorm(x2, norm_g[l], BF16)
        pg_lo = _inproj(h, w_lo, dm.lo_blocks)
        pg_hi = _inproj(h, w_hi, dm.hi_gate)
        c_rows = _fgate(h, wf, bf, dm).reshape(dm.batch * 8, 1, dm.seq)
        y_a = _dilated(pg_lo, dm)
        y_b = _conv(pg_lo, conv_w[l], conv_b[l], conv_ln_g[l], conv_ln_b[l],
                    conv_pw[l].astype(BF16), conv_pw_b[l], dm)
        y_c = _fox(pg_lo, pg_hi, c_rows, dm)
        y_d = _pool(pg_hi, pool_w[l].astype(BF16), pool_scale[l], dm)
        merged = _merge(pg_hi, (y_a, y_b, y_c, y_d), ws, dm)
        x2 = _outproj(merged, w_out[l].astype(BF16), x2)

    return _rmsnorm(x2, final_g, x.dtype).reshape(x.shape)
```

```python
import functools
import math
from typing import NamedTuple

import jax
import jax.numpy as jnp
from jax import lax
from jax.experimental import pallas as pl
from jax.experimental.pallas import tpu as pltpu

F32 = jnp.float32
BF16 = jnp.bfloat16

EPS = 1e-6
HEAD_DIM = 128
LANE = 128
DILATED_CFG = ((128, 1), (512, 4), (2048, 16))
BAND = 128
CONV_K = 31
CONV_HALO = 32
POOL_SIZES = (2, 4, 8, 16)
POOL_HALO = 16
VMEM_LIMIT = 56 * 1024 * 1024
LOG2E = math.log2(math.e)


class _Dims(NamedTuple):
    batch: int
    seq: int
    d_model: int
    hpg: int
    w4: int
    fox_heads: int
    lo_aq: int
    lo_ak: int
    lo_av: int
    lo_ag: int
    lo_ba: int
    lo_bb: int
    lo_bg: int
    lo_cq: int
    lo_ck: int
    lo_cv: int
    lo_blocks: int
    hi_cg: int
    hi_di: int
    hi_dg: int
    hi_gate: int
    hi_blocks: int


def _dims(x_shape):
    batch, seq, d = x_shape
    hpg = d // 1024
    w4 = d // 512
    lo_ba = 10 * hpg
    lo_cq = lo_ba + 3 * w4
    return _Dims(batch=batch, seq=seq, d_model=d, hpg=hpg, w4=w4, fox_heads=d // 512,
                 lo_aq=0, lo_ak=3 * hpg, lo_av=6 * hpg, lo_ag=9 * hpg,
                 lo_ba=lo_ba, lo_bb=lo_ba + w4, lo_bg=lo_ba + 2 * w4,
                 lo_cq=lo_cq, lo_ck=lo_cq + w4, lo_cv=lo_cq + 2 * w4, lo_blocks=lo_cq + 3 * w4,
                 hi_cg=0, hi_di=w4, hi_dg=2 * w4, hi_gate=3 * w4,
                 hi_blocks=3 * w4 + 4 * (d // LANE))


def _params(*sem):
    return pltpu.CompilerParams(dimension_semantics=sem, vmem_limit_bytes=VMEM_LIMIT)


def _silu(x):
    return x * jax.nn.sigmoid(x)


def _rmsnorm_kernel(x_ref, g_ref, o_ref):
    x = x_ref[...]
    ms = jnp.mean(x * x, axis=-1, keepdims=True)
    o_ref[...] = (x * lax.rsqrt(ms + EPS) * g_ref[...]).astype(o_ref.dtype)


def _rmsnorm(x2, g, out_dtype):
    m, d = x2.shape
    tm = min(512, m)
    return pl.pallas_call(
        _rmsnorm_kernel,
        grid=(m // tm,),
        in_specs=[pl.BlockSpec((tm, d), lambda i: (i, 0)),
                  pl.BlockSpec((1, d), lambda i: (0, 0))],
        out_specs=pl.BlockSpec((tm, d), lambda i: (i, 0)),
        out_shape=jax.ShapeDtypeStruct((m, d), out_dtype),
        compiler_params=_params("parallel"),
        name="rmsnorm",
    )(x2, g.reshape(1, d))


def _inproj_kernel(h_ref, w_ref, o_ref, *, first_gate_tile):
    acc = jnp.dot(h_ref[...], w_ref[...], preferred_element_type=F32)
    j = pl.program_id(1)

    @pl.when(j < first_gate_tile)
    def _():
        o_ref[...] = acc.astype(o_ref.dtype)

    @pl.when(j >= first_gate_tile)
    def _():
        o_ref[...] = jax.nn.sigmoid(acc).astype(o_ref.dtype)


def _col_tile(n, *aligned):
    return 1024 if all(v % 1024 == 0 for v in (n, *aligned)) else 512


def _inproj(h, w, layer, first_gate_block):
    m, d = h.shape
    n = w.shape[2]
    tm = min(1024, m)
    tn = _col_tile(n, first_gate_block * LANE)
    assert n % tn == 0 and (first_gate_block * LANE) % tn == 0
    return pl.pallas_call(
        functools.partial(_inproj_kernel, first_gate_tile=first_gate_block * LANE // tn),
        grid=(m // tm, n // tn),
        in_specs=[pl.BlockSpec((tm, d), lambda i, j: (i, 0)),
                  pl.BlockSpec((None, d, tn), lambda i, j: (layer, 0, j))],
        out_specs=pl.BlockSpec((tm, tn), lambda i, j: (i, j)),
        out_shape=jax.ShapeDtypeStruct((m, n), BF16),
        compiler_params=_params("parallel", "parallel"),
        name="inproj",
    )(h, w)


def _cast_kernel(x_ref, o_ref):
    o_ref[...] = x_ref[...].astype(o_ref.dtype)


def _cast_shift_kernel(x_ref, nxt_ref, o_ref, *, shift):
    x = jnp.concatenate([x_ref[...], nxt_ref[...]], axis=1)
    o_ref[...] = x[:, shift:shift + o_ref.shape[1]].astype(o_ref.dtype)


def _cast_w_in(w_in, dm):
    depth, d, n_in = w_in.shape
    n_lo = dm.lo_blocks * LANE
    n_hi = dm.hi_blocks * LANE
    shift = dm.fox_heads
    assert n_in == n_lo + shift + n_hi and shift < LANE
    tr = min(512, d)
    tc = _col_tile(n_lo, n_hi)
    w_lo = pl.pallas_call(
        _cast_kernel,
        grid=(depth, d // tr, n_lo // tc),
        in_specs=[pl.BlockSpec((None, tr, tc), lambda l, i, j: (l, i, j))],
        out_specs=pl.BlockSpec((None, tr, tc), lambda l, i, j: (l, i, j)),
        out_shape=jax.ShapeDtypeStruct((depth, d, n_lo), BF16),
        compiler_params=_params("parallel", "parallel", "parallel"),
        name="cast_w_lo",
    )(w_in)
    lo_tiles = n_lo // tc
    lane_blocks = tc // LANE
    w_hi = pl.pallas_call(
        functools.partial(_cast_shift_kernel, shift=shift),
        grid=(depth, d // tr, n_hi // tc),
        in_specs=[pl.BlockSpec((None, tr, tc), lambda l, i, j: (l, i, lo_tiles + j)),
                  pl.BlockSpec((None, tr, LANE),
                               lambda l, i, j: (l, i, (lo_tiles + j + 1) * lane_blocks))],
        out_specs=pl.BlockSpec((None, tr, tc), lambda l, i, j: (l, i, j)),
        out_shape=jax.ShapeDtypeStruct((depth, d, n_hi), BF16),
        compiler_params=_params("parallel", "parallel", "parallel"),
        name="cast_w_hi",
    )(w_in, w_in)
    return w_lo, w_hi


def _fgate_kernel(h_ref, wf_ref, bf_ref, o_ref, carry_ref, *, tile):
    @pl.when(pl.program_id(1) == 0)
    def _():
        carry_ref[...] = jnp.zeros_like(carry_ref)

    f = jnp.dot(h_ref[...], wf_ref[...], preferred_element_type=F32) + bf_ref[...]
    ls = jnp.minimum(f, 0.0) - jnp.log1p(jnp.exp(-jnp.abs(f)))
    row = lax.broadcasted_iota(jnp.int32, (tile, tile), 0)
    col = lax.broadcasted_iota(jnp.int32, (tile, tile), 1)
    tri = jnp.where(row >= col, 1.0, 0.0).astype(BF16)
    hi = ls.astype(BF16)
    r1 = ls - hi.astype(F32)
    mid = r1.astype(BF16)
    lo = (r1 - mid.astype(F32)).astype(BF16)
    c = (jnp.dot(tri, hi, preferred_element_type=F32)
         + jnp.dot(tri, mid, preferred_element_type=F32)
         + jnp.dot(tri, lo, preferred_element_type=F32)) + carry_ref[...]
    carry_ref[...] = c[tile - 1:tile, :]
    o_ref[0] = c.T[0:8, :]


def _fgate(h, wf, bf, dm):
    m, d = h.shape
    tile = 512
    nt = dm.seq // tile
    assert dm.fox_heads <= 8
    return pl.pallas_call(
        functools.partial(_fgate_kernel, tile=tile),
        grid=(dm.batch, nt),
        in_specs=[pl.BlockSpec((tile, d), lambda b, t: (b * nt + t, 0)),
                  pl.BlockSpec((d, LANE), lambda b, t: (0, 0)),
                  pl.BlockSpec((1, LANE), lambda b, t: (0, 0))],
        out_specs=pl.BlockSpec((1, 8, tile), lambda b, t: (b, 0, t)),
        out_shape=jax.ShapeDtypeStruct((dm.batch, 8, dm.seq), F32),
        scratch_shapes=[pltpu.VMEM((1, LANE), F32)],
        compiler_params=_params("parallel", "arbitrary"),
        name="fgate",
    )(h, wf, bf)


def _dilated_kernel(q0, k0, v0, q1, k1, v1, q2, k2, v2, gate_ref, o_ref, *, seq):
    nblk = seq // BAND
    scale = HEAD_DIM ** -0.5
    qi = lax.broadcasted_iota(jnp.int32, (BAND, BAND), 0)
    kc = lax.broadcasted_iota(jnp.int32, (BAND, BAND), 1)

    groups = []
    for refs, (window, dil) in zip(((q0, k0, v0), (q1, k1, v1), (q2, k2, v2)), DILATED_CFG):
        same = ((qi - kc) & (dil - 1)) == 0
        oldest = jnp.where(same & (kc >= qi), 0.0, -jnp.inf)
        middle = jnp.where(same, 0.0, -jnp.inf)
        diag = jnp.where(same & (kc <= qi), 0.0, -jnp.inf)
        groups.append((refs, window // BAND, oldest, middle, diag))

    for n in range(nblk):
        rows = slice(n * BAND, (n + 1) * BAND)
        outs, lses = [], []
        for (q_ref, k_ref, v_ref), wblk, oldest, middle, diag in groups:
            first = max(0, n - wblk)
            tiles = [oldest if m == n - wblk else middle for m in range(first, n)] + [diag]
            bias = tiles[0] if len(tiles) == 1 else jnp.concatenate(tiles, axis=1)
            keys = slice(first * BAND, (n + 1) * BAND)
            s = lax.dot_general(q_ref[rows, :], k_ref[keys, :], (((1,), (1,)), ((), ())),
                                preferred_element_type=F32) * scale + bias
            m = jnp.max(s, axis=-1, keepdims=True)
            p = jnp.exp(s - m)
            l = jnp.sum(p, axis=-1, keepdims=True)
            outs.append(jnp.dot(p.astype(BF16), v_ref[keys, :], preferred_element_type=F32) / l)
            lses.append(m + jnp.log(l))
        mx = jnp.maximum(jnp.maximum(lses[0], lses[1]), lses[2])
        e = [jnp.exp(a - mx) for a in lses]
        o = (e[0] * outs[0] + e[1] * outs[1] + e[2] * outs[2]) / (e[0] + e[1] + e[2])
        o_ref[rows, :] = (o * _silu(gate_ref[rows, :].astype(F32))).astype(o_ref.dtype)


def _dilated(pg_lo, dm):
    seq = dm.seq
    assert seq % BAND == 0 and all(d & (d - 1) == 0 for _, d in DILATED_CFG)

    def col_spec(cb, g):
        return pl.BlockSpec((seq, HEAD_DIM), lambda b, j, cb=cb, g=g: (b, cb + g * dm.hpg + j))

    in_specs = []
    for g in range(len(DILATED_CFG)):
        in_specs += [col_spec(dm.lo_aq, g), col_spec(dm.lo_ak, g), col_spec(dm.lo_av, g)]
    in_specs.append(pl.BlockSpec((seq, HEAD_DIM), lambda b, j: (b, dm.lo_ag + j)))
    return pl.pallas_call(
        functools.partial(_dilated_kernel, seq=seq),
        grid=(dm.batch, dm.hpg),
        in_specs=in_specs,
        out_specs=pl.BlockSpec((seq, HEAD_DIM), lambda b, j: (b, j)),
        out_shape=jax.ShapeDtypeStruct((dm.batch * seq, dm.hpg * HEAD_DIM), BF16),
        compiler_params=_params("parallel", "parallel"),
        name="dilated_attn",
    )(*([pg_lo] * 10))


def _conv_kernel(a_ref, b_ref, ah_ref, bh_ref, gate_ref, cw_ref, cb_ref, lg_ref, lb_ref,
                 pw_ref, pwb_ref, o_ref, ybuf, cbuf, *, tile, rows):
    width = ybuf.shape[1]
    first = pl.program_id(1) == 0
    ybuf[CONV_HALO:CONV_HALO + tile, :] = (
        a_ref[...].astype(F32) * jax.nn.sigmoid(b_ref[...].astype(F32)))
    halo = ah_ref[...].astype(F32) * jax.nn.sigmoid(bh_ref[...].astype(F32))
    ybuf[0:CONV_HALO, :] = jnp.where(first, 0.0, halo)

    shift = CONV_HALO - (CONV_K - 1)
    for c in range(width // LANE):
        cs = slice(c * LANE, (c + 1) * LANE)
        taps = [cw_ref[j:j + 1, cs] for j in range(CONV_K)]
        bias = cb_ref[:, cs]
        for rc in range(tile // rows):
            acc = jnp.zeros((rows, LANE), F32)
            for j in range(CONV_K):
                start = rc * rows + shift + j
                acc = acc + taps[j] * ybuf[start:start + rows, cs]
            cbuf[rc * rows:(rc + 1) * rows, cs] = acc + bias

    y = cbuf[...]
    mu = jnp.mean(y, axis=-1, keepdims=True)
    yc = y - mu
    var = jnp.mean(yc * yc, axis=-1, keepdims=True)
    yn = yc * lax.rsqrt(var + EPS) * lg_ref[...] + lb_ref[...]
    z = _silu(yn).astype(BF16)
    out = jnp.dot(z, pw_ref[...], preferred_element_type=F32) + pwb_ref[...]
    o_ref[...] = (out * _silu(gate_ref[...].astype(F32))).astype(o_ref.dtype)


def _conv(pg_lo, cw, cb, lg, lb, pw, pwb, dm):
    width = dm.w4 * LANE
    tile = 512
    nt = dm.seq // tile
    hb = tile // CONV_HALO

    def halo_rows(b, i):
        return jnp.maximum((b * nt + i) * hb - 1, 0)

    ca, cbb, cg = dm.lo_ba // dm.w4, dm.lo_bb // dm.w4, dm.lo_bg // dm.w4
    vec = pl.BlockSpec((1, width), lambda b, i: (0, 0))
    return pl.pallas_call(
        functools.partial(_conv_kernel, tile=tile, rows=128),
        grid=(dm.batch, nt),
        in_specs=[pl.BlockSpec((tile, width), lambda b, i: (b * nt + i, ca)),
                  pl.BlockSpec((tile, width), lambda b, i: (b * nt + i, cbb)),
                  pl.BlockSpec((CONV_HALO, width), lambda b, i: (halo_rows(b, i), ca)),
                  pl.BlockSpec((CONV_HALO, width), lambda b, i: (halo_rows(b, i), cbb)),
                  pl.BlockSpec((tile, width), lambda b, i: (b * nt + i, cg)),
                  pl.BlockSpec((CONV_K, width), lambda b, i: (0, 0)),
                  vec, vec, vec,
                  pl.BlockSpec((width, width), lambda b, i: (0, 0)),
                  vec],
        out_specs=pl.BlockSpec((tile, width), lambda b, i: (b * nt + i, 0)),
        out_shape=jax.ShapeDtypeStruct((dm.batch * dm.seq, width), BF16),
        scratch_shapes=[pltpu.VMEM((CONV_HALO + tile, width), F32),
                        pltpu.VMEM((tile, width), F32)],
        compiler_params=_params("parallel", "parallel"),
        name="conformer_conv",
    )(pg_lo, pg_lo, pg_lo, pg_lo, pg_lo, cw, cb.reshape(1, width), lg.reshape(1, width),
      lb.reshape(1, width), pw, pwb.reshape(1, width))


def _fox_kernel(q_ref, k_ref, v_ref, gate_ref, c_ref, o_ref, *, tq):
    i = pl.program_id(2)
    q = q_ref[...]
    scale2 = HEAD_DIM ** -0.5 * LOG2E

    def scores(j):
        off = pl.multiple_of(j * tq, tq)
        kb = k_ref[pl.ds(off, tq), :]
        s = lax.dot_general(q, kb, (((1,), (1,)), ((), ())), preferred_element_type=F32)
        return s * scale2 + c_ref[0, :, pl.ds(off, tq)] * (-LOG2E), off

    def update(s, off, carry):
        m, l, acc = carry
        m_new = jnp.maximum(m, jnp.max(s, axis=-1, keepdims=True))
        alpha = jnp.exp2(m - m_new)
        p = jnp.exp2(s - m_new)
        l = alpha * l + jnp.sum(p, axis=-1, keepdims=True)
        vb = v_ref[pl.ds(off, tq), :]
        acc = alpha * acc + jnp.dot(p.astype(BF16), vb, preferred_element_type=F32)
        return m_new, l, acc

    def below_diagonal(j, carry):
        s, off = scores(j)
        return update(s, off, carry)

    init = (jnp.full((tq, 1), -jnp.inf, F32), jnp.zeros((tq, 1), F32),
            jnp.zeros((tq, HEAD_DIM), F32))
    carry = lax.fori_loop(0, i, below_diagonal, init)
    s, off = scores(i)
    row = lax.broadcasted_iota(jnp.int32, (tq, tq), 0)
    col = lax.broadcasted_iota(jnp.int32, (tq, tq), 1)
    _, l, acc = update(jnp.where(col <= row, s, -jnp.inf), off, carry)
    o_ref[...] = (acc / l * _silu(gate_ref[...].astype(F32))).astype(o_ref.dtype)


def _fox(pg_lo, pg_hi, c_rows, dm):
    seq, heads = dm.seq, dm.fox_heads
    tq = 512
    nq = seq // tq
    return pl.pallas_call(
        functools.partial(_fox_kernel, tq=tq),
        grid=(dm.batch, heads, nq),
        in_specs=[pl.BlockSpec((tq, HEAD_DIM), lambda b, h, i: (b * nq + i, dm.lo_cq + h)),
                  pl.BlockSpec((seq, HEAD_DIM), lambda b, h, i: (b, dm.lo_ck + h)),
                  pl.BlockSpec((seq, HEAD_DIM), lambda b, h, i: (b, dm.lo_cv + h)),
                  pl.BlockSpec((tq, HEAD_DIM), lambda b, h, i: (b * nq + i, dm.hi_cg + h)),
                  pl.BlockSpec((1, 1, seq), lambda b, h, i: (b * 8 + h, 0, 0))],
        out_specs=pl.BlockSpec((tq, HEAD_DIM), lambda b, h, i: (b * nq + i, h)),
        out_shape=jax.ShapeDtypeStruct((dm.batch * seq, heads * HEAD_DIM), BF16),
        compiler_params=_params("parallel", "parallel", "parallel"),
        name="fox_attn",
    )(pg_lo, pg_lo, pg_lo, pg_hi, c_rows)


def _pool_kernel(x_ref, xh_ref, gate_ref, pw_ref, ps_ref, o_ref, xbuf, *, tile):
    i = pl.program_id(1)
    xbuf[POOL_HALO:POOL_HALO + tile, :] = x_ref[...].astype(F32)
    xbuf[0:POOL_HALO, :] = jnp.where(i == 0, 0.0, xh_ref[...].astype(F32))
    gc = xbuf.shape[1] // len(POOL_SIZES)
    t = i * tile + lax.broadcasted_iota(jnp.int32, (tile, 1), 0)
    for g, p in enumerate(POOL_SIZES):
        cs = slice(g * gc, (g + 1) * gc)
        x = xbuf[POOL_HALO:POOL_HALO + tile, cs]
        win = x
        for k in range(1, p):
            win = win + xbuf[POOL_HALO - k:POOL_HALO - k + tile, cs]
        cnt = jnp.minimum(t + 1, p).astype(F32)
        d = (win / cnt - x).astype(BF16)
        y = jnp.dot(d, pw_ref[g], preferred_element_type=F32) * ps_ref[:, cs]
        o_ref[:, cs] = (y * _silu(gate_ref[:, cs].astype(F32))).astype(o_ref.dtype)


def _pool(pg_hi, pw, ps, dm):
    width = dm.w4 * LANE
    tile = 512
    nt = dm.seq // tile
    hb = tile // POOL_HALO
    gc = width // len(POOL_SIZES)
    ci, cg = dm.hi_di // dm.w4, dm.hi_dg // dm.w4
    return pl.pallas_call(
        functools.partial(_pool_kernel, tile=tile),
        grid=(dm.batch, nt),
        in_specs=[pl.BlockSpec((tile, width), lambda b, i: (b * nt + i, ci)),
                  pl.BlockSpec((POOL_HALO, width),
                               lambda b, i: (jnp.maximum((b * nt + i) * hb - 1, 0), ci)),
                  pl.BlockSpec((tile, width), lambda b, i: (b * nt + i, cg)),
                  pl.BlockSpec((len(POOL_SIZES), gc, gc), lambda b, i: (0, 0, 0)),
                  pl.BlockSpec((1, width), lambda b, i: (0, 0))],
        out_specs=pl.BlockSpec((tile, width), lambda b, i: (b * nt + i, 0)),
        out_shape=jax.ShapeDtypeStruct((dm.batch * dm.seq, width), BF16),
        scratch_shapes=[pltpu.VMEM((POOL_HALO + tile, width), F32)],
        compiler_params=_params("parallel", "parallel"),
        name="pool_mixer",
    )(pg_hi, pg_hi, pg_hi, pw, ps.reshape(1, width))


def _merge_kernel(ga, gb, gc, gd, ya, yb, yc, yd, wa, wb, wc, wd, o_ref):
    acc = None
    for g, y, w in ((ga, ya, wa), (gb, yb, wb), (gc, yc, wc), (gd, yd, wd)):
        u = jnp.dot(y[...], w[...], preferred_element_type=F32)
        term = g[...].astype(F32) * u
        acc = term if acc is None else acc + term
    o_ref[...] = acc.astype(o_ref.dtype)


def _merge(pg_hi, ys, ws, dm):
    m = dm.batch * dm.seq
    d = dm.d_model
    tm = min(1024, m)
    tn = 512
    gate0 = dm.hi_gate * LANE // tn
    per_gate = d // tn
    in_specs = [pl.BlockSpec((tm, tn), lambda i, j, k=k: (i, gate0 + k * per_gate + j))
                for k in range(4)]
    in_specs += [pl.BlockSpec((tm, y.shape[1]), lambda i, j: (i, 0)) for y in ys]
    in_specs += [pl.BlockSpec((w.shape[0], tn), lambda i, j: (0, j)) for w in ws]
    return pl.pallas_call(
        _merge_kernel,
        grid=(m // tm, d // tn),
        in_specs=in_specs,
        out_specs=pl.BlockSpec((tm, tn), lambda i, j: (i, j)),
        out_shape=jax.ShapeDtypeStruct((m, d), BF16),
        compiler_params=_params("parallel", "parallel"),
        name="gated_merge",
    )(pg_hi, pg_hi, pg_hi, pg_hi, *ys, *ws)


def _outproj_kernel(m_ref, w_ref, x_ref, o_ref):
    o_ref[...] = x_ref[...] + jnp.dot(m_ref[...], w_ref[...], preferred_element_type=F32)


def _outproj(merged, w_out, x2):
    m, d = x2.shape
    tm = min(1024, m)
    tn = 512
    return pl.pallas_call(
        _outproj_kernel,
        grid=(m // tm, d // tn),
        in_specs=[pl.BlockSpec((tm, d), lambda i, j: (i, 0)),
                  pl.BlockSpec((d, tn), lambda i, j: (0, j)),
                  pl.BlockSpec((tm, tn), lambda i, j: (i, j))],
        out_specs=pl.BlockSpec((tm, tn), lambda i, j: (i, j)),
        out_shape=jax.ShapeDtypeStruct((m, d), F32),
        compiler_params=_params("parallel", "parallel"),
        name="outproj",
    )(merged, w_out, x2)


def kernel(x, norm_g, w_in, b_forget, conv_w, conv_b, conv_ln_g, conv_ln_b, conv_pw, conv_pw_b,
           pool_w, pool_scale, w_branch, w_out, final_g):
    dm = _dims(x.shape)
    depth = norm_g.shape[0]
    x2 = x.reshape(dm.batch * dm.seq, dm.d_model)
    heads = dm.fox_heads
    cf_col = dm.lo_blocks * LANE
    row_off = (0, dm.hpg * LANE, (dm.hpg + dm.w4) * LANE, (dm.hpg + 2 * dm.w4) * LANE,
               (dm.hpg + 3 * dm.w4) * LANE)

    w_lo, w_hi = _cast_w_in(w_in, dm)
    for l in range(depth):
        wf = jnp.pad(w_in[l, :, cf_col:cf_col + heads], ((0, 0), (0, LANE - heads))).astype(BF16)
        bf = jnp.pad(b_forget[l], (0, LANE - heads)).reshape(1, LANE)
        ws = [w_branch[l, row_off[k]:row_off[k + 1]].astype(BF16) for k in range(4)]

        h = _rmsnorm(x2, norm_g[l], BF16)
        pg_lo = _inproj(h, w_lo, l, dm.lo_blocks)
        pg_hi = _inproj(h, w_hi, l, dm.hi_gate)
        c_rows = _fgate(h, wf, bf, dm).reshape(dm.batch * 8, 1, dm.seq)
        y_a = _dilated(pg_lo, dm)
        y_b = _conv(pg_lo, conv_w[l], conv_b[l], conv_ln_g[l], conv_ln_b[l],
                    conv_pw[l].astype(BF16), conv_pw_b[l], dm)
        y_c = _fox(pg_lo, pg_hi, c_rows, dm)
        y_d = _pool(pg_hi, pool_w[l].astype(BF16), pool_scale[l], dm)
        merged = _merge(pg_hi, (y_a, y_b, y_c, y_d), ws, dm)
        x2 = _outproj(merged, w_out[l].astype(BF16), x2)

    return _rmsnorm(x2, final_g, x.dtype).reshape(x.shape)
```

```python
import functools
import math
from typing import NamedTuple

import jax
import jax.numpy as jnp
from jax import lax
from jax.experimental import pallas as pl
from jax.experimental.pallas import tpu as pltpu

F32 = jnp.float32
BF16 = jnp.bfloat16

EPS = 1e-6
HEAD_DIM = 128
LANE = 128
SUBLANE = 8
DILATED_CFG = ((128, 1), (512, 4), (2048, 16))
BAND = 128
CONV_K = 31
CONV_HALO = 32
POOL_SIZES = (2, 4, 8, 16)
POOL_HALO = 16
VMEM_LIMIT = 56 * 1024 * 1024
LOG2E = math.log2(math.e)


class _Dims(NamedTuple):
    batch: int
    seq: int
    d_model: int
    hpg: int
    w4: int
    fox_heads: int
    lo_aq: int
    lo_ak: int
    lo_av: int
    lo_ag: int
    lo_ba: int
    lo_bb: int
    lo_bg: int
    lo_cq: int
    lo_ck: int
    lo_cv: int
    lo_blocks: int
    hi_cg: int
    hi_di: int
    hi_dg: int
    hi_gate: int
    hi_blocks: int


def _dims(x_shape):
    batch, seq, d = x_shape
    hpg = d // 1024
    w4 = d // 512
    lo_ba = 10 * hpg
    lo_cq = lo_ba + 3 * w4
    return _Dims(batch=batch, seq=seq, d_model=d, hpg=hpg, w4=w4, fox_heads=d // 512,
                 lo_aq=0, lo_ak=3 * hpg, lo_av=6 * hpg, lo_ag=9 * hpg,
                 lo_ba=lo_ba, lo_bb=lo_ba + w4, lo_bg=lo_ba + 2 * w4,
                 lo_cq=lo_cq, lo_ck=lo_cq + w4, lo_cv=lo_cq + 2 * w4, lo_blocks=lo_cq + 3 * w4,
                 hi_cg=0, hi_di=w4, hi_dg=2 * w4, hi_gate=3 * w4,
                 hi_blocks=3 * w4 + 4 * (d // LANE))


def _params(*sem):
    return pltpu.CompilerParams(dimension_semantics=sem, vmem_limit_bytes=VMEM_LIMIT)


def _silu(x):
    return x * jax.nn.sigmoid(x)


def _rmsnorm_kernel(x_ref, g_ref, o_ref):
    x = x_ref[...]
    ms = jnp.mean(x * x, axis=-1, keepdims=True)
    o_ref[...] = (x * lax.rsqrt(ms + EPS) * g_ref[...]).astype(o_ref.dtype)


def _rmsnorm(x2, g, out_dtype):
    m, d = x2.shape
    tm = min(512, m)
    return pl.pallas_call(
        _rmsnorm_kernel,
        grid=(m // tm,),
        in_specs=[pl.BlockSpec((tm, d), lambda i: (i, 0)),
                  pl.BlockSpec((1, d), lambda i: (0, 0))],
        out_specs=pl.BlockSpec((tm, d), lambda i: (i, 0)),
        out_shape=jax.ShapeDtypeStruct((m, d), out_dtype),
        compiler_params=_params("parallel"),
        name="rmsnorm",
    )(x2, g.reshape(1, d))


def _inproj_kernel(h_ref, w_ref, o_ref, *, first_gate_tile):
    acc = jnp.dot(h_ref[...], w_ref[...], preferred_element_type=F32)
    j = pl.program_id(1)

    @pl.when(j < first_gate_tile)
    def _():
        o_ref[...] = acc.astype(o_ref.dtype)

    @pl.when(j >= first_gate_tile)
    def _():
        o_ref[...] = jax.nn.sigmoid(acc).astype(o_ref.dtype)


def _col_tile(n, *aligned):
    return 1024 if all(v % 1024 == 0 for v in (n, *aligned)) else 512


def _inproj(h, w, layer, n, first_gate_block):
    m, d = h.shape
    tm = min(1024, m)
    tn = _col_tile(n, first_gate_block * LANE)
    assert n % tn == 0 and (first_gate_block * LANE) % tn == 0 and n <= w.shape[2]
    return pl.pallas_call(
        functools.partial(_inproj_kernel, first_gate_tile=first_gate_block * LANE // tn),
        grid=(m // tm, n // tn),
        in_specs=[pl.BlockSpec((tm, d), lambda i, j: (i, 0)),
                  pl.BlockSpec((None, d, tn), lambda i, j: (layer, 0, j))],
        out_specs=pl.BlockSpec((tm, tn), lambda i, j: (i, j)),
        out_shape=jax.ShapeDtypeStruct((m, n), BF16),
        compiler_params=_params("parallel", "parallel"),
        name="inproj",
    )(h, w)


def _fgate_kernel(h_ref, wf_ref, bf_ref, o_ref, carry_ref, *, tile):
    @pl.when(pl.program_id(1) == 0)
    def _():
        carry_ref[...] = jnp.zeros_like(carry_ref)

    f = jnp.dot(h_ref[...], wf_ref[...], preferred_element_type=F32) + bf_ref[...]
    ls = jnp.minimum(f, 0.0) - jnp.log1p(jnp.exp(-jnp.abs(f)))
    row = lax.broadcasted_iota(jnp.int32, (tile, tile), 0)
    col = lax.broadcasted_iota(jnp.int32, (tile, tile), 1)
    tri = jnp.where(row >= col, 1.0, 0.0).astype(BF16)
    hi = ls.astype(BF16)
    r1 = ls - hi.astype(F32)
    mid = r1.astype(BF16)
    lo = (r1 - mid.astype(F32)).astype(BF16)
    c = (jnp.dot(tri, hi, preferred_element_type=F32)
         + jnp.dot(tri, mid, preferred_element_type=F32)
         + jnp.dot(tri, lo, preferred_element_type=F32)) + carry_ref[...]
    carry_ref[...] = c[tile - 1:tile, :]
    o_ref[0] = c.T[0:8, :]


def _fgate(h, wf, bf, dm):
    m, d = h.shape
    tile = 512
    nt = dm.seq // tile
    assert dm.fox_heads <= 8
    return pl.pallas_call(
        functools.partial(_fgate_kernel, tile=tile),
        grid=(dm.batch, nt),
        in_specs=[pl.BlockSpec((tile, d), lambda b, t: (b * nt + t, 0)),
                  pl.BlockSpec((d, LANE), lambda b, t: (0, 0)),
                  pl.BlockSpec((1, LANE), lambda b, t: (0, 0))],
        out_specs=pl.BlockSpec((1, 8, tile), lambda b, t: (b, 0, t)),
        out_shape=jax.ShapeDtypeStruct((dm.batch, 8, dm.seq), F32),
        scratch_shapes=[pltpu.VMEM((1, LANE), F32)],
        compiler_params=_params("parallel", "arbitrary"),
        name="fgate",
    )(h, wf, bf)


def _dilated_kernel(q0, k0, v0, q1, k1, v1, q2, k2, v2, gate_ref, o_ref, *, seq):
    nblk = seq // BAND
    scale = HEAD_DIM ** -0.5
    qi = lax.broadcasted_iota(jnp.int32, (BAND, BAND), 0)
    kc = lax.broadcasted_iota(jnp.int32, (BAND, BAND), 1)

    groups = []
    for refs, (window, dil) in zip(((q0, k0, v0), (q1, k1, v1), (q2, k2, v2)), DILATED_CFG):
        same = ((qi - kc) & (dil - 1)) == 0
        oldest = jnp.where(same & (kc >= qi), 0.0, -jnp.inf)
        middle = jnp.where(same, 0.0, -jnp.inf)
        diag = jnp.where(same & (kc <= qi), 0.0, -jnp.inf)
        groups.append((refs, window // BAND, oldest, middle, diag))

    for n in range(nblk):
        rows = slice(n * BAND, (n + 1) * BAND)
        outs, lses = [], []
        for (q_ref, k_ref, v_ref), wblk, oldest, middle, diag in groups:
            first = max(0, n - wblk)
            tiles = [oldest if m == n - wblk else middle for m in range(first, n)] + [diag]
            bias = tiles[0] if len(tiles) == 1 else jnp.concatenate(tiles, axis=1)
            keys = slice(first * BAND, (n + 1) * BAND)
            s = lax.dot_general(q_ref[rows, :], k_ref[keys, :], (((1,), (1,)), ((), ())),
                                preferred_element_type=F32) * scale + bias
            m = jnp.max(s, axis=-1, keepdims=True)
            p = jnp.exp(s - m)
            l = jnp.sum(p, axis=-1, keepdims=True)
            outs.append(jnp.dot(p.astype(BF16), v_ref[keys, :], preferred_element_type=F32) / l)
            lses.append(m + jnp.log(l))
        mx = jnp.maximum(jnp.maximum(lses[0], lses[1]), lses[2])
        e = [jnp.exp(a - mx) for a in lses]
        o = (e[0] * outs[0] + e[1] * outs[1] + e[2] * outs[2]) / (e[0] + e[1] + e[2])
        o_ref[rows, :] = (o * _silu(gate_ref[rows, :].astype(F32))).astype(o_ref.dtype)


def _dilated(pg_lo, dm):
    seq = dm.seq
    assert seq % BAND == 0 and all(d & (d - 1) == 0 for _, d in DILATED_CFG)

    def col_spec(cb, g):
        return pl.BlockSpec((seq, HEAD_DIM), lambda b, j, cb=cb, g=g: (b, cb + g * dm.hpg + j))

    in_specs = []
    for g in range(len(DILATED_CFG)):
        in_specs += [col_spec(dm.lo_aq, g), col_spec(dm.lo_ak, g), col_spec(dm.lo_av, g)]
    in_specs.append(pl.BlockSpec((seq, HEAD_DIM), lambda b, j: (b, dm.lo_ag + j)))
    return pl.pallas_call(
        functools.partial(_dilated_kernel, seq=seq),
        grid=(dm.batch, dm.hpg),
        in_specs=in_specs,
        out_specs=pl.BlockSpec((seq, HEAD_DIM), lambda b, j: (b, j)),
        out_shape=jax.ShapeDtypeStruct((dm.batch * seq, dm.hpg * HEAD_DIM), BF16),
        compiler_params=_params("parallel", "parallel"),
        name="dilated_attn",
    )(*([pg_lo] * 10))


def _conv_kernel(a_ref, b_ref, ah_ref, bh_ref, gate_ref, cw_ref, cb_ref, lg_ref, lb_ref,
                 pw_ref, pwb_ref, o_ref, ybuf, zbuf, cbuf, *, tile, rows):
    width = ybuf.shape[1]
    first = pl.program_id(1) == 0
    ybuf[CONV_HALO:CONV_HALO + tile, :] = (
        a_ref[...].astype(F32) * jax.nn.sigmoid(b_ref[...].astype(F32)))
    halo = ah_ref[...].astype(F32) * jax.nn.sigmoid(bh_ref[...].astype(F32))
    ybuf[0:CONV_HALO, :] = jnp.where(first, 0.0, halo)

    total = CONV_HALO + tile
    for c in range(width // LANE):
        cs = slice(c * LANE, (c + 1) * LANE)
        for b in range(1, SUBLANE):
            zbuf[b - 1, SUBLANE:total, :] = ybuf[SUBLANE - b:total - b, cs]
        taps = [cw_ref[j:j + 1, cs] for j in range(CONV_K)]
        bias = cb_ref[:, cs]
        for rc in range(tile // rows):
            acc = jnp.zeros((rows, LANE), F32)
            for lag in range(CONV_K):
                a, b = divmod(lag, SUBLANE)
                start = CONV_HALO + rc * rows - SUBLANE * a
                src = (ybuf[start:start + rows, cs] if b == 0
                       else zbuf[b - 1, start:start + rows, :])
                acc = acc + taps[CONV_K - 1 - lag] * src
            cbuf[rc * rows:(rc + 1) * rows, cs] = acc + bias

    y = cbuf[...]
    mu = jnp.mean(y, axis=-1, keepdims=True)
    yc = y - mu
    var = jnp.mean(yc * yc, axis=-1, keepdims=True)
    yn = yc * lax.rsqrt(var + EPS) * lg_ref[...] + lb_ref[...]
    z = _silu(yn).astype(BF16)
    out = jnp.dot(z, pw_ref[...], preferred_element_type=F32) + pwb_ref[...]
    o_ref[...] = (out * _silu(gate_ref[...].astype(F32))).astype(o_ref.dtype)


def _conv(pg_lo, cw, cb, lg, lb, pw, pwb, dm):
    width = dm.w4 * LANE
    tile = 512
    nt = dm.seq // tile
    hb = tile // CONV_HALO

    def halo_rows(b, i):
        return jnp.maximum((b * nt + i) * hb - 1, 0)

    ca, cbb, cg = dm.lo_ba // dm.w4, dm.lo_bb // dm.w4, dm.lo_bg // dm.w4
    vec = pl.BlockSpec((1, width), lambda b, i: (0, 0))
    return pl.pallas_call(
        functools.partial(_conv_kernel, tile=tile, rows=128),
        grid=(dm.batch, nt),
        in_specs=[pl.BlockSpec((tile, width), lambda b, i: (b * nt + i, ca)),
                  pl.BlockSpec((tile, width), lambda b, i: (b * nt + i, cbb)),
                  pl.BlockSpec((CONV_HALO, width), lambda b, i: (halo_rows(b, i), ca)),
                  pl.BlockSpec((CONV_HALO, width), lambda b, i: (halo_rows(b, i), cbb)),
                  pl.BlockSpec((tile, width), lambda b, i: (b * nt + i, cg)),
                  pl.BlockSpec((CONV_K, width), lambda b, i: (0, 0)),
                  vec, vec, vec,
                  pl.BlockSpec((width, width), lambda b, i: (0, 0)),
                  vec],
        out_specs=pl.BlockSpec((tile, width), lambda b, i: (b * nt + i, 0)),
        out_shape=jax.ShapeDtypeStruct((dm.batch * dm.seq, width), BF16),
        scratch_shapes=[pltpu.VMEM((CONV_HALO + tile, width), F32),
                        pltpu.VMEM((SUBLANE - 1, CONV_HALO + tile, LANE), F32),
                        pltpu.VMEM((tile, width), F32)],
        compiler_params=_params("parallel", "parallel"),
        name="conformer_conv",
    )(pg_lo, pg_lo, pg_lo, pg_lo, pg_lo, cw, cb.reshape(1, width), lg.reshape(1, width),
      lb.reshape(1, width), pw, pwb.reshape(1, width))


def _fox_kernel(q_ref, k_ref, v_ref, gate_ref, c_ref, o_ref, *, tq):
    i = pl.program_id(2)
    q = q_ref[...]
    scale2 = HEAD_DIM ** -0.5 * LOG2E

    def scores(j):
        off = pl.multiple_of(j * tq, tq)
        kb = k_ref[pl.ds(off, tq), :]
        s = lax.dot_general(q, kb, (((1,), (1,)), ((), ())), preferred_element_type=F32)
        return s * scale2 + c_ref[0, :, pl.ds(off, tq)] * (-LOG2E), off

    def update(s, off, carry):
        m, l, acc = carry
        m_new = jnp.maximum(m, jnp.max(s, axis=-1, keepdims=True))
        alpha = jnp.exp2(m - m_new)
        p = jnp.exp2(s - m_new)
        l = alpha * l + jnp.sum(p, axis=-1, keepdims=True)
        vb = v_ref[pl.ds(off, tq), :]
        acc = alpha * acc + jnp.dot(p.astype(BF16), vb, preferred_element_type=F32)
        return m_new, l, acc

    def below_diagonal(j, carry):
        s, off = scores(j)
        return update(s, off, carry)

    init = (jnp.full((tq, 1), -jnp.inf, F32), jnp.zeros((tq, 1), F32),
            jnp.zeros((tq, HEAD_DIM), F32))
    carry = lax.fori_loop(0, i, below_diagonal, init)
    s, off = scores(i)
    row = lax.broadcasted_iota(jnp.int32, (tq, tq), 0)
    col = lax.broadcasted_iota(jnp.int32, (tq, tq), 1)
    _, l, acc = update(jnp.where(col <= row, s, -jnp.inf), off, carry)
    o_ref[...] = (acc / l * _silu(gate_ref[...].astype(F32))).astype(o_ref.dtype)


def _fox(pg_lo, pg_hi, c_rows, dm):
    seq, heads = dm.seq, dm.fox_heads
    tq = 512
    nq = seq // tq
    return pl.pallas_call(
        functools.partial(_fox_kernel, tq=tq),
        grid=(dm.batch, heads, nq),
        in_specs=[pl.BlockSpec((tq, HEAD_DIM), lambda b, h, i: (b * nq + i, dm.lo_cq + h)),
                  pl.BlockSpec((seq, HEAD_DIM), lambda b, h, i: (b, dm.lo_ck + h)),
                  pl.BlockSpec((seq, HEAD_DIM), lambda b, h, i: (b, dm.lo_cv + h)),
                  pl.BlockSpec((tq, HEAD_DIM), lambda b, h, i: (b * nq + i, dm.hi_cg + h)),
                  pl.BlockSpec((1, 1, seq), lambda b, h, i: (b * 8 + h, 0, 0))],
        out_specs=pl.BlockSpec((tq, HEAD_DIM), lambda b, h, i: (b * nq + i, h)),
        out_shape=jax.ShapeDtypeStruct((dm.batch * seq, heads * HEAD_DIM), BF16),
        compiler_params=_params("parallel", "parallel", "parallel"),
        name="fox_attn",
    )(pg_lo, pg_lo, pg_lo, pg_hi, c_rows)


def _pool_kernel(x_ref, xh_ref, gate_ref, pw_ref, ps_ref, o_ref, xbuf, *, tile):
    i = pl.program_id(1)
    xbuf[POOL_HALO:POOL_HALO + tile, :] = x_ref[...].astype(F32)
    xbuf[0:POOL_HALO, :] = jnp.where(i == 0, 0.0, xh_ref[...].astype(F32))
    gc = xbuf.shape[1] // len(POOL_SIZES)
    t = i * tile + lax.broadcasted_iota(jnp.int32, (tile, 1), 0)
    for g, p in enumerate(POOL_SIZES):
        cs = slice(g * gc, (g + 1) * gc)
        x = xbuf[POOL_HALO:POOL_HALO + tile, cs]
        win = x
        for k in range(1, p):
            win = win + xbuf[POOL_HALO - k:POOL_HALO - k + tile, cs]
        cnt = jnp.minimum(t + 1, p).astype(F32)
        d = (win / cnt - x).astype(BF16)
        y = jnp.dot(d, pw_ref[g], preferred_element_type=F32) * ps_ref[:, cs]
        o_ref[:, cs] = (y * _silu(gate_ref[:, cs].astype(F32))).astype(o_ref.dtype)


def _pool(pg_hi, pw, ps, dm):
    width = dm.w4 * LANE
    tile = 512
    nt = dm.seq // tile
    hb = tile // POOL_HALO
    gc = width // len(POOL_SIZES)
    ci, cg = dm.hi_di // dm.w4, dm.hi_dg // dm.w4
    return pl.pallas_call(
        functools.partial(_pool_kernel, tile=tile),
        grid=(dm.batch, nt),
        in_specs=[pl.BlockSpec((tile, width), lambda b, i: (b * nt + i, ci)),
                  pl.BlockSpec((POOL_HALO, width),
                               lambda b, i: (jnp.maximum((b * nt + i) * hb - 1, 0), ci)),
                  pl.BlockSpec((tile, width), lambda b, i: (b * nt + i, cg)),
                  pl.BlockSpec((len(POOL_SIZES), gc, gc), lambda b, i: (0, 0, 0)),
                  pl.BlockSpec((1, width), lambda b, i: (0, 0))],
        out_specs=pl.BlockSpec((tile, width), lambda b, i: (b * nt + i, 0)),
        out_shape=jax.ShapeDtypeStruct((dm.batch * dm.seq, width), BF16),
        scratch_shapes=[pltpu.VMEM((POOL_HALO + tile, width), F32)],
        compiler_params=_params("parallel", "parallel"),
        name="pool_mixer",
    )(pg_hi, pg_hi, pg_hi, pw, ps.reshape(1, width))


def _merge_kernel(ga, gb, gc, gd, ya, yb, yc, yd, wa, wb, wc, wd, o_ref):
    acc = None
    for g, y, w in ((ga, ya, wa), (gb, yb, wb), (gc, yc, wc), (gd, yd, wd)):
        u = jnp.dot(y[...], w[...], preferred_element_type=F32)
        term = g[...].astype(F32) * u
        acc = term if acc is None else acc + term
    o_ref[...] = acc.astype(o_ref.dtype)


def _merge(pg_hi, ys, ws, dm):
    m = dm.batch * dm.seq
    d = dm.d_model
    tm = min(1024, m)
    tn = 512
    gate0 = dm.hi_gate * LANE // tn
    per_gate = d // tn
    in_specs = [pl.BlockSpec((tm, tn), lambda i, j, k=k: (i, gate0 + k * per_gate + j))
                for k in range(4)]
    in_specs += [pl.BlockSpec((tm, y.shape[1]), lambda i, j: (i, 0)) for y in ys]
    in_specs += [pl.BlockSpec((w.shape[0], tn), lambda i, j: (0, j)) for w in ws]
    return pl.pallas_call(
        _merge_kernel,
        grid=(m // tm, d // tn),
        in_specs=in_specs,
        out_specs=pl.BlockSpec((tm, tn), lambda i, j: (i, j)),
        out_shape=jax.ShapeDtypeStruct((m, d), BF16),
        compiler_params=_params("parallel", "parallel"),
        name="gated_merge",
    )(pg_hi, pg_hi, pg_hi, pg_hi, *ys, *ws)


def _outproj_kernel(m_ref, w_ref, x_ref, o_ref):
    o_ref[...] = x_ref[...] + jnp.dot(m_ref[...], w_ref[...], preferred_element_type=F32)


def _outproj(merged, w_out, x2):
    m, d = x2.shape
    tm = min(1024, m)
    tn = 512
    return pl.pallas_call(
        _outproj_kernel,
        grid=(m // tm, d // tn),
        in_specs=[pl.BlockSpec((tm, d), lambda i, j: (i, 0)),
                  pl.BlockSpec((d, tn), lambda i, j: (0, j)),
                  pl.BlockSpec((tm, tn), lambda i, j: (i, j))],
        out_specs=pl.BlockSpec((tm, tn), lambda i, j: (i, j)),
        out_shape=jax.ShapeDtypeStruct((m, d), F32),
        compiler_params=_params("parallel", "parallel"),
        name="outproj",
    )(merged, w_out, x2)


def kernel(x, norm_g, w_in, b_forget, conv_w, conv_b, conv_ln_g, conv_ln_b, conv_pw, conv_pw_b,
           pool_w, pool_scale, w_branch, w_out, final_g):
    dm = _dims(x.shape)
    depth = norm_g.shape[0]
    x2 = x.reshape(dm.batch * dm.seq, dm.d_model)
    heads = dm.fox_heads
    cf_col = dm.lo_blocks * LANE
    row_off = (0, dm.hpg * LANE, (dm.hpg + dm.w4) * LANE, (dm.hpg + 2 * dm.w4) * LANE,
               (dm.hpg + 3 * dm.w4) * LANE)

    w_bf = w_in.astype(BF16)
    w_hi = w_bf[:, :, cf_col + heads:]
    for l in range(depth):
        wf = jnp.pad(w_bf[l, :, cf_col:cf_col + heads], ((0, 0), (0, LANE - heads)))
        bf = jnp.pad(b_forget[l], (0, LANE - heads)).reshape(1, LANE)
        ws = [w_branch[l, row_off[k]:row_off[k + 1]].astype(BF16) for k in range(4)]

        h = _rmsnorm(x2, norm_g[l], BF16)
        pg_lo = _inproj(h, w_bf, l, cf_col, dm.lo_blocks)
        pg_hi = _inproj(h, w_hi, l, dm.hi_blocks * LANE, dm.hi_gate)
        c_rows = _fgate(h, wf, bf, dm).reshape(dm.batch * 8, 1, dm.seq)
        y_a = _dilated(pg_lo, dm)
        y_b = _conv(pg_lo, conv_w[l], conv_b[l], conv_ln_g[l], conv_ln_b[l],
                    conv_pw[l].astype(BF16), conv_pw_b[l], dm)
        y_c = _fox(pg_lo, pg_hi, c_rows, dm)
        y_d = _pool(pg_hi, pool_w[l].astype(BF16), pool_scale[l], dm)
        merged = _merge(pg_hi, (y_a, y_b, y_c, y_d), ws, dm)
        x2 = _outproj(merged, w_out[l].astype(BF16), x2)

    return _rmsnorm(x2, final_g, x.dtype).reshape(x.shape)
```

```python
import functools
import math
from typing import NamedTuple

import jax
import jax.numpy as jnp
from jax import lax
from jax.experimental import pallas as pl
from jax.experimental.pallas import tpu as pltpu

F32 = jnp.float32
BF16 = jnp.bfloat16

EPS = 1e-6
HEAD_DIM = 128
LANE = 128
SUBLANE = 8
INPROJ_CHUNK = 256
DILATED_CFG = ((128, 1), (512, 4), (2048, 16))
BAND = 128
CONV_K = 31
CONV_HALO = 32
POOL_SIZES = (2, 4, 8, 16)
POOL_HALO = 16
VMEM_LIMIT = 56 * 1024 * 1024
LOG2E = math.log2(math.e)


class _Dims(NamedTuple):
    batch: int
    seq: int
    d_model: int
    hpg: int
    w4: int
    fox_heads: int
    lo_aq: int
    lo_ak: int
    lo_av: int
    lo_ag: int
    lo_ba: int
    lo_bb: int
    lo_bg: int
    lo_cq: int
    lo_ck: int
    lo_cv: int
    lo_blocks: int
    hi_cg: int
    hi_di: int
    hi_dg: int
    hi_gate: int
    hi_blocks: int


def _dims(x_shape):
    batch, seq, d = x_shape
    hpg = d // 1024
    w4 = d // 512
    lo_ba = 10 * hpg
    lo_cq = lo_ba + 3 * w4
    return _Dims(batch=batch, seq=seq, d_model=d, hpg=hpg, w4=w4, fox_heads=d // 512,
                 lo_aq=0, lo_ak=3 * hpg, lo_av=6 * hpg, lo_ag=9 * hpg,
                 lo_ba=lo_ba, lo_bb=lo_ba + w4, lo_bg=lo_ba + 2 * w4,
                 lo_cq=lo_cq, lo_ck=lo_cq + w4, lo_cv=lo_cq + 2 * w4, lo_blocks=lo_cq + 3 * w4,
                 hi_cg=0, hi_di=w4, hi_dg=2 * w4, hi_gate=3 * w4,
                 hi_blocks=3 * w4 + 4 * (d // LANE))


def _params(*sem):
    return pltpu.CompilerParams(dimension_semantics=sem, vmem_limit_bytes=VMEM_LIMIT)


def _silu(x):
    return x * jax.nn.sigmoid(x)


def _rmsnorm_kernel(x_ref, g_ref, o_ref):
    x = x_ref[...]
    ms = jnp.mean(x * x, axis=-1, keepdims=True)
    o_ref[...] = (x * lax.rsqrt(ms + EPS) * g_ref[...]).astype(o_ref.dtype)


def _rmsnorm(x2, g, out_dtype):
    m, d = x2.shape
    tm = min(512, m)
    return pl.pallas_call(
        _rmsnorm_kernel,
        grid=(m // tm,),
        in_specs=[pl.BlockSpec((tm, d), lambda i: (i, 0)),
                  pl.BlockSpec((1, d), lambda i: (0, 0))],
        out_specs=pl.BlockSpec((tm, d), lambda i: (i, 0)),
        out_shape=jax.ShapeDtypeStruct((m, d), out_dtype),
        compiler_params=_params("parallel"),
        name="rmsnorm",
    )(x2, g.reshape(1, d))


def _inproj_kernel(h_ref, w_ref, o_ref, *, gate, chunk):
    for c in range(o_ref.shape[1] // chunk):
        cs = slice(c * chunk, (c + 1) * chunk)
        acc = jnp.dot(h_ref[...], w_ref[:, cs], preferred_element_type=F32)
        o_ref[:, cs] = (jax.nn.sigmoid(acc) if gate else acc).astype(o_ref.dtype)


def _col_tile(*aligned):
    return 1024 if all(v % 1024 == 0 for v in aligned) else 512


def _inproj(h, w, layer, col0, n, gate):
    m, d = h.shape
    tm = min(1024, m)
    tn = _col_tile(col0, n)
    assert n % tn == 0 and col0 % tn == 0 and col0 + n <= w.shape[2]
    first = col0 // tn
    return pl.pallas_call(
        functools.partial(_inproj_kernel, gate=gate, chunk=INPROJ_CHUNK),
        grid=(m // tm, n // tn),
        in_specs=[pl.BlockSpec((tm, d), lambda i, j: (i, 0)),
                  pl.BlockSpec((None, d, tn), lambda i, j: (layer, 0, first + j))],
        out_specs=pl.BlockSpec((tm, tn), lambda i, j: (i, j)),
        out_shape=jax.ShapeDtypeStruct((m, n), BF16),
        compiler_params=_params("parallel", "parallel"),
        name="inproj_gate" if gate else "inproj",
    )(h, w)


def _fgate_kernel(h_ref, wf_ref, bf_ref, o_ref, carry_ref, *, tile):
    @pl.when(pl.program_id(1) == 0)
    def _():
        carry_ref[...] = jnp.zeros_like(carry_ref)

    f = jnp.dot(h_ref[...], wf_ref[...], preferred_element_type=F32) + bf_ref[...]
    ls = jnp.minimum(f, 0.0) - jnp.log1p(jnp.exp(-jnp.abs(f)))
    row = lax.broadcasted_iota(jnp.int32, (tile, tile), 0)
    col = lax.broadcasted_iota(jnp.int32, (tile, tile), 1)
    tri = jnp.where(row >= col, 1.0, 0.0).astype(BF16)
    hi = ls.astype(BF16)
    r1 = ls - hi.astype(F32)
    mid = r1.astype(BF16)
    lo = (r1 - mid.astype(F32)).astype(BF16)
    c = (jnp.dot(tri, hi, preferred_element_type=F32)
         + jnp.dot(tri, mid, preferred_element_type=F32)
         + jnp.dot(tri, lo, preferred_element_type=F32)) + carry_ref[...]
    carry_ref[...] = c[tile - 1:tile, :]
    o_ref[0] = c.T[0:8, :]


def _fgate(h, wf, bf, dm):
    m, d = h.shape
    tile = 512
    nt = dm.seq // tile
    assert dm.fox_heads <= 8
    return pl.pallas_call(
        functools.partial(_fgate_kernel, tile=tile),
        grid=(dm.batch, nt),
        in_specs=[pl.BlockSpec((tile, d), lambda b, t: (b * nt + t, 0)),
                  pl.BlockSpec((d, LANE), lambda b, t: (0, 0)),
                  pl.BlockSpec((1, LANE), lambda b, t: (0, 0))],
        out_specs=pl.BlockSpec((1, 8, tile), lambda b, t: (b, 0, t)),
        out_shape=jax.ShapeDtypeStruct((dm.batch, 8, dm.seq), F32),
        scratch_shapes=[pltpu.VMEM((1, LANE), F32)],
        compiler_params=_params("parallel", "arbitrary"),
        name="fgate",
    )(h, wf, bf)


def _dilated_kernel(q0, k0, v0, q1, k1, v1, q2, k2, v2, gate_ref, o_ref, *, seq):
    nblk = seq // BAND
    scale = HEAD_DIM ** -0.5
    qi = lax.broadcasted_iota(jnp.int32, (BAND, BAND), 0)
    kc = lax.broadcasted_iota(jnp.int32, (BAND, BAND), 1)

    groups = []
    for refs, (window, dil) in zip(((q0, k0, v0), (q1, k1, v1), (q2, k2, v2)), DILATED_CFG):
        same = ((qi - kc) & (dil - 1)) == 0
        oldest = jnp.where(same & (kc >= qi), 0.0, -jnp.inf)
        middle = jnp.where(same, 0.0, -jnp.inf)
        diag = jnp.where(same & (kc <= qi), 0.0, -jnp.inf)
        groups.append((refs, window // BAND, oldest, middle, diag))

    for n in range(nblk):
        rows = slice(n * BAND, (n + 1) * BAND)
        outs, lses = [], []
        for (q_ref, k_ref, v_ref), wblk, oldest, middle, diag in groups:
            first = max(0, n - wblk)
            tiles = [oldest if m == n - wblk else middle for m in range(first, n)] + [diag]
            bias = tiles[0] if len(tiles) == 1 else jnp.concatenate(tiles, axis=1)
            keys = slice(first * BAND, (n + 1) * BAND)
            s = lax.dot_general(q_ref[rows, :], k_ref[keys, :], (((1,), (1,)), ((), ())),
                                preferred_element_type=F32) * scale + bias
            m = jnp.max(s, axis=-1, keepdims=True)
            p = jnp.exp(s - m)
            l = jnp.sum(p, axis=-1, keepdims=True)
            outs.append(jnp.dot(p.astype(BF16), v_ref[keys, :], preferred_element_type=F32) / l)
            lses.append(m + jnp.log(l))
        mx = jnp.maximum(jnp.maximum(lses[0], lses[1]), lses[2])
        e = [jnp.exp(a - mx) for a in lses]
        o = (e[0] * outs[0] + e[1] * outs[1] + e[2] * outs[2]) / (e[0] + e[1] + e[2])
        o_ref[rows, :] = (o * _silu(gate_ref[rows, :].astype(F32))).astype(o_ref.dtype)


def _dilated(pg_lo, dm):
    seq = dm.seq
    assert seq % BAND == 0 and all(d & (d - 1) == 0 for _, d in DILATED_CFG)

    def col_spec(cb, g):
        return pl.BlockSpec((seq, HEAD_DIM), lambda b, j, cb=cb, g=g: (b, cb + g * dm.hpg + j))

    in_specs = []
    for g in range(len(DILATED_CFG)):
        in_specs += [col_spec(dm.lo_aq, g), col_spec(dm.lo_ak, g), col_spec(dm.lo_av, g)]
    in_specs.append(pl.BlockSpec((seq, HEAD_DIM), lambda b, j: (b, dm.lo_ag + j)))
    return pl.pallas_call(
        functools.partial(_dilated_kernel, seq=seq),
        grid=(dm.batch, dm.hpg),
        in_specs=in_specs,
        out_specs=pl.BlockSpec((seq, HEAD_DIM), lambda b, j: (b, j)),
        out_shape=jax.ShapeDtypeStruct((dm.batch * seq, dm.hpg * HEAD_DIM), BF16),
        compiler_params=_params("parallel", "parallel"),
        name="dilated_attn",
    )(*([pg_lo] * 10))


def _conv_kernel(a_ref, b_ref, ah_ref, bh_ref, gate_ref, cw_ref, cb_ref, lg_ref, lb_ref,
                 pw_ref, pwb_ref, o_ref, ybuf, zbuf, cbuf, *, tile, rows):
    width = ybuf.shape[1]
    first = pl.program_id(1) == 0
    ybuf[CONV_HALO:CONV_HALO + tile, :] = (
        a_ref[...].astype(F32) * jax.nn.sigmoid(b_ref[...].astype(F32)))
    halo = ah_ref[...].astype(F32) * jax.nn.sigmoid(bh_ref[...].astype(F32))
    ybuf[0:CONV_HALO, :] = jnp.where(first, 0.0, halo)

    total = CONV_HALO + tile
    for c in range(width // LANE):
        cs = slice(c * LANE, (c + 1) * LANE)
        for b in range(1, SUBLANE):
            zbuf[b - 1, SUBLANE:total, :] = ybuf[SUBLANE - b:total - b, cs]
        taps = [cw_ref[j:j + 1, cs] for j in range(CONV_K)]
        bias = cb_ref[:, cs]
        for rc in range(tile // rows):
            acc = jnp.zeros((rows, LANE), F32)
            for lag in range(CONV_K):
                a, b = divmod(lag, SUBLANE)
                start = CONV_HALO + rc * rows - SUBLANE * a
                src = (ybuf[start:start + rows, cs] if b == 0
                       else zbuf[b - 1, start:start + rows, :])
                acc = acc + taps[CONV_K - 1 - lag] * src
            cbuf[rc * rows:(rc + 1) * rows, cs] = acc + bias

    y = cbuf[...]
    mu = jnp.mean(y, axis=-1, keepdims=True)
    yc = y - mu
    var = jnp.mean(yc * yc, axis=-1, keepdims=True)
    yn = yc * lax.rsqrt(var + EPS) * lg_ref[...] + lb_ref[...]
    z = _silu(yn).astype(BF16)
    out = jnp.dot(z, pw_ref[...], preferred_element_type=F32) + pwb_ref[...]
    o_ref[...] = (out * _silu(gate_ref[...].astype(F32))).astype(o_ref.dtype)


def _conv(pg_lo, cw, cb, lg, lb, pw, pwb, dm):
    width = dm.w4 * LANE
    tile = 512
    nt = dm.seq // tile
    hb = tile // CONV_HALO

    def halo_rows(b, i):
        return jnp.maximum((b * nt + i) * hb - 1, 0)

    ca, cbb, cg = dm.lo_ba // dm.w4, dm.lo_bb // dm.w4, dm.lo_bg // dm.w4
    vec = pl.BlockSpec((1, width), lambda b, i: (0, 0))
    return pl.pallas_call(
        functools.partial(_conv_kernel, tile=tile, rows=128),
        grid=(dm.batch, nt),
        in_specs=[pl.BlockSpec((tile, width), lambda b, i: (b * nt + i, ca)),
                  pl.BlockSpec((tile, width), lambda b, i: (b * nt + i, cbb)),
                  pl.BlockSpec((CONV_HALO, width), lambda b, i: (halo_rows(b, i), ca)),
                  pl.BlockSpec((CONV_HALO, width), lambda b, i: (halo_rows(b, i), cbb)),
                  pl.BlockSpec((tile, width), lambda b, i: (b * nt + i, cg)),
                  pl.BlockSpec((CONV_K, width), lambda b, i: (0, 0)),
                  vec, vec, vec,
                  pl.BlockSpec((width, width), lambda b, i: (0, 0)),
                  vec],
        out_specs=pl.BlockSpec((tile, width), lambda b, i: (b * nt + i, 0)),
        out_shape=jax.ShapeDtypeStruct((dm.batch * dm.seq, width), BF16),
        scratch_shapes=[pltpu.VMEM((CONV_HALO + tile, width), F32),
                        pltpu.VMEM((SUBLANE - 1, CONV_HALO + tile, LANE), F32),
                        pltpu.VMEM((tile, width), F32)],
        compiler_params=_params("parallel", "parallel"),
        name="conformer_conv",
    )(pg_lo, pg_lo, pg_lo, pg_lo, pg_lo, cw, cb.reshape(1, width), lg.reshape(1, width),
      lb.reshape(1, width), pw, pwb.reshape(1, width))


def _fox_kernel(q_ref, k_ref, v_ref, gate_ref, c_ref, o_ref, *, tq):
    i = pl.program_id(2)
    q = q_ref[...]
    scale2 = HEAD_DIM ** -0.5 * LOG2E

    def scores(j):
        off = pl.multiple_of(j * tq, tq)
        kb = k_ref[pl.ds(off, tq), :]
        s = lax.dot_general(q, kb, (((1,), (1,)), ((), ())), preferred_element_type=F32)
        return s * scale2 + c_ref[0, :, pl.ds(off, tq)] * (-LOG2E), off

    def update(s, off, carry):
        m, l, acc = carry
        m_new = jnp.maximum(m, jnp.max(s, axis=-1, keepdims=True))
        alpha = jnp.exp2(m - m_new)
        p = jnp.exp2(s - m_new)
        l = alpha * l + jnp.sum(p, axis=-1, keepdims=True)
        vb = v_ref[pl.ds(off, tq), :]
        acc = alpha * acc + jnp.dot(p.astype(BF16), vb, preferred_element_type=F32)
        return m_new, l, acc

    def below_diagonal(j, carry):
        s, off = scores(j)
        return update(s, off, carry)

    init = (jnp.full((tq, 1), -jnp.inf, F32), jnp.zeros((tq, 1), F32),
            jnp.zeros((tq, HEAD_DIM), F32))
    carry = lax.fori_loop(0, i, below_diagonal, init)
    s, off = scores(i)
    row = lax.broadcasted_iota(jnp.int32, (tq, tq), 0)
    col = lax.broadcasted_iota(jnp.int32, (tq, tq), 1)
    _, l, acc = update(jnp.where(col <= row, s, -jnp.inf), off, carry)
    o_ref[...] = (acc / l * _silu(gate_ref[...].astype(F32))).astype(o_ref.dtype)


def _fox(pg_lo, pg_hi, c_rows, dm):
    seq, heads = dm.seq, dm.fox_heads
    tq = 512
    nq = seq // tq
    return pl.pallas_call(
        functools.partial(_fox_kernel, tq=tq),
        grid=(dm.batch, heads, nq),
        in_specs=[pl.BlockSpec((tq, HEAD_DIM), lambda b, h, i: (b * nq + i, dm.lo_cq + h)),
                  pl.BlockSpec((seq, HEAD_DIM), lambda b, h, i: (b, dm.lo_ck + h)),
                  pl.BlockSpec((seq, HEAD_DIM), lambda b, h, i: (b, dm.lo_cv + h)),
                  pl.BlockSpec((tq, HEAD_DIM), lambda b, h, i: (b * nq + i, dm.hi_cg + h)),
                  pl.BlockSpec((1, 1, seq), lambda b, h, i: (b * 8 + h, 0, 0))],
        out_specs=pl.BlockSpec((tq, HEAD_DIM), lambda b, h, i: (b * nq + i, h)),
        out_shape=jax.ShapeDtypeStruct((dm.batch * seq, heads * HEAD_DIM), BF16),
        compiler_params=_params("parallel", "parallel", "parallel"),
        name="fox_attn",
    )(pg_lo, pg_lo, pg_lo, pg_hi, c_rows)


def _pool_kernel(x_ref, xh_ref, gate_ref, pw_ref, ps_ref, o_ref, xbuf, *, tile):
    i = pl.program_id(1)
    xbuf[POOL_HALO:POOL_HALO + tile, :] = x_ref[...].astype(F32)
    xbuf[0:POOL_HALO, :] = jnp.where(i == 0, 0.0, xh_ref[...].astype(F32))
    gc = xbuf.shape[1] // len(POOL_SIZES)
    t = i * tile + lax.broadcasted_iota(jnp.int32, (tile, 1), 0)
    for g, p in enumerate(POOL_SIZES):
        cs = slice(g * gc, (g + 1) * gc)
        x = xbuf[POOL_HALO:POOL_HALO + tile, cs]
        win = x
        for k in range(1, p):
            win = win + xbuf[POOL_HALO - k:POOL_HALO - k + tile, cs]
        cnt = jnp.minimum(t + 1, p).astype(F32)
        d = (win / cnt - x).astype(BF16)
        y = jnp.dot(d, pw_ref[g], preferred_element_type=F32) * ps_ref[:, cs]
        o_ref[:, cs] = (y * _silu(gate_ref[:, cs].astype(F32))).astype(o_ref.dtype)


def _pool(pg_hi, pw, ps, dm):
    width = dm.w4 * LANE
    tile = 512
    nt = dm.seq // tile
    hb = tile // POOL_HALO
    gc = width // len(POOL_SIZES)
    ci, cg = dm.hi_di // dm.w4, dm.hi_dg // dm.w4
    return pl.pallas_call(
        functools.partial(_pool_kernel, tile=tile),
        grid=(dm.batch, nt),
        in_specs=[pl.BlockSpec((tile, width), lambda b, i: (b * nt + i, ci)),
                  pl.BlockSpec((POOL_HALO, width),
                               lambda b, i: (jnp.maximum((b * nt + i) * hb - 1, 0), ci)),
                  pl.BlockSpec((tile, width), lambda b, i: (b * nt + i, cg)),
                  pl.BlockSpec((len(POOL_SIZES), gc, gc), lambda b, i: (0, 0, 0)),
                  pl.BlockSpec((1, width), lambda b, i: (0, 0))],
        out_specs=pl.BlockSpec((tile, width), lambda b, i: (b * nt + i, 0)),
        out_shape=jax.ShapeDtypeStruct((dm.batch * dm.seq, width), BF16),
        scratch_shapes=[pltpu.VMEM((POOL_HALO + tile, width), F32)],
        compiler_params=_params("parallel", "parallel"),
        name="pool_mixer",
    )(pg_hi, pg_hi, pg_hi, pw, ps.reshape(1, width))


def _merge_kernel(ga, gb, gc, gd, ya, yb, yc, yd, wa, wb, wc, wd, o_ref):
    acc = None
    for g, y, w in ((ga, ya, wa), (gb, yb, wb), (gc, yc, wc), (gd, yd, wd)):
        u = jnp.dot(y[...], w[...], preferred_element_type=F32)
        term = g[...].astype(F32) * u
        acc = term if acc is None else acc + term
    o_ref[...] = acc.astype(o_ref.dtype)


def _merge(gates, ys, ws, dm):
    m = dm.batch * dm.seq
    d = dm.d_model
    tm = min(1024, m)
    tn = 512
    per_gate = d // tn
    in_specs = [pl.BlockSpec((tm, tn), lambda i, j, k=k: (i, k * per_gate + j))
                for k in range(4)]
    in_specs += [pl.BlockSpec((tm, y.shape[1]), lambda i, j: (i, 0)) for y in ys]
    in_specs += [pl.BlockSpec((w.shape[0], tn), lambda i, j: (0, j)) for w in ws]
    return pl.pallas_call(
        _merge_kernel,
        grid=(m // tm, d // tn),
        in_specs=in_specs,
        out_specs=pl.BlockSpec((tm, tn), lambda i, j: (i, j)),
        out_shape=jax.ShapeDtypeStruct((m, d), BF16),
        compiler_params=_params("parallel", "parallel"),
        name="gated_merge",
    )(gates, gates, gates, gates, *ys, *ws)


def _outproj_kernel(m_ref, w_ref, x_ref, o_ref):
    o_ref[...] = x_ref[...] + jnp.dot(m_ref[...], w_ref[...], preferred_element_type=F32)


def _outproj(merged, w_out, x2):
    m, d = x2.shape
    tm = min(1024, m)
    tn = 512
    return pl.pallas_call(
        _outproj_kernel,
        grid=(m // tm, d // tn),
        in_specs=[pl.BlockSpec((tm, d), lambda i, j: (i, 0)),
                  pl.BlockSpec((d, tn), lambda i, j: (0, j)),
                  pl.BlockSpec((tm, tn), lambda i, j: (i, j))],
        out_specs=pl.BlockSpec((tm, tn), lambda i, j: (i, j)),
        out_shape=jax.ShapeDtypeStruct((m, d), F32),
        compiler_params=_params("parallel", "parallel"),
        name="outproj",
    )(merged, w_out, x2)


def kernel(x, norm_g, w_in, b_forget, conv_w, conv_b, conv_ln_g, conv_ln_b, conv_pw, conv_pw_b,
           pool_w, pool_scale, w_branch, w_out, final_g):
    dm = _dims(x.shape)
    depth = norm_g.shape[0]
    x2 = x.reshape(dm.batch * dm.seq, dm.d_model)
    heads = dm.fox_heads
    cf_col = dm.lo_blocks * LANE
    row_off = (0, dm.hpg * LANE, (dm.hpg + dm.w4) * LANE, (dm.hpg + 2 * dm.w4) * LANE,
               (dm.hpg + 3 * dm.w4) * LANE)

    w_bf = w_in.astype(BF16)
    w_hi = w_bf[:, :, cf_col + heads:]
    for l in range(depth):
        wf = jnp.pad(w_bf[l, :, cf_col:cf_col + heads], ((0, 0), (0, LANE - heads)))
        bf = jnp.pad(b_forget[l], (0, LANE - heads)).reshape(1, LANE)
        ws = [w_branch[l, row_off[k]:row_off[k + 1]].astype(BF16) for k in range(4)]

        h = _rmsnorm(x2, norm_g[l], BF16)
        pg_lo = _inproj(h, w_bf, l, 0, cf_col, False)
        pg_hi = _inproj(h, w_hi, l, 0, dm.hi_gate * LANE, False)
        gates = _inproj(h, w_hi, l, dm.hi_gate * LANE, 4 * dm.d_model, True)
        c_rows = _fgate(h, wf, bf, dm).reshape(dm.batch * 8, 1, dm.seq)
        y_a = _dilated(pg_lo, dm)
        y_b = _conv(pg_lo, conv_w[l], conv_b[l], conv_ln_g[l], conv_ln_b[l],
                    conv_pw[l].astype(BF16), conv_pw_b[l], dm)
        y_c = _fox(pg_lo, pg_hi, c_rows, dm)
        y_d = _pool(pg_hi, pool_w[l].astype(BF16), pool_scale[l], dm)
        merged = _merge(gates, (y_a, y_b, y_c, y_d), ws, dm)
        x2 = _outproj(merged, w_out[l].astype(BF16), x2)

    return _rmsnorm(x2, final_g, x.dtype).reshape(x.shape)
```

```python
import functools
import math
from typing import NamedTuple

import jax
import jax.numpy as jnp
from jax import lax
from jax.experimental import pallas as pl
from jax.experimental.pallas import tpu as pltpu

F32 = jnp.float32
BF16 = jnp.bfloat16

EPS = 1e-6
HEAD_DIM = 128
LANE = 128
SUBLANE = 8
MXU_COLS = 256
DILATED_CFG = ((128, 1), (512, 4), (2048, 16))
BAND = 128
CONV_K = 31
CONV_HALO = 32
POOL_SIZES = (2, 4, 8, 16)
POOL_HALO = 16
VMEM_LIMIT = 56 * 1024 * 1024
LOG2E = math.log2(math.e)


class _Dims(NamedTuple):
    batch: int
    seq: int
    d_model: int
    hpg: int
    w4: int
    fox_heads: int
    lo_aq: int
    lo_ak: int
    lo_av: int
    lo_ag: int
    lo_ba: int
    lo_bb: int
    lo_bg: int
    lo_cq: int
    lo_ck: int
    lo_cv: int
    lo_blocks: int
    hi_cg: int
    hi_di: int
    hi_dg: int
    hi_gate: int
    hi_blocks: int


def _dims(x_shape):
    batch, seq, d = x_shape
    hpg = d // 1024
    w4 = d // 512
    lo_ba = 10 * hpg
    lo_cq = lo_ba + 3 * w4
    return _Dims(batch=batch, seq=seq, d_model=d, hpg=hpg, w4=w4, fox_heads=d // 512,
                 lo_aq=0, lo_ak=3 * hpg, lo_av=6 * hpg, lo_ag=9 * hpg,
                 lo_ba=lo_ba, lo_bb=lo_ba + w4, lo_bg=lo_ba + 2 * w4,
                 lo_cq=lo_cq, lo_ck=lo_cq + w4, lo_cv=lo_cq + 2 * w4, lo_blocks=lo_cq + 3 * w4,
                 hi_cg=0, hi_di=w4, hi_dg=2 * w4, hi_gate=3 * w4,
                 hi_blocks=3 * w4 + 4 * (d // LANE))


def _params(*sem):
    return pltpu.CompilerParams(dimension_semantics=sem, vmem_limit_bytes=VMEM_LIMIT)


def _silu(x):
    return x * jax.nn.sigmoid(x)


def _rmsnorm_kernel(x_ref, g_ref, o_ref):
    x = x_ref[...]
    ms = jnp.mean(x * x, axis=-1, keepdims=True)
    o_ref[...] = (x * lax.rsqrt(ms + EPS) * g_ref[...]).astype(o_ref.dtype)


def _rmsnorm(x2, g, out_dtype):
    m, d = x2.shape
    tm = min(512, m)
    return pl.pallas_call(
        _rmsnorm_kernel,
        grid=(m // tm,),
        in_specs=[pl.BlockSpec((tm, d), lambda i: (i, 0)),
                  pl.BlockSpec((1, d), lambda i: (0, 0))],
        out_specs=pl.BlockSpec((tm, d), lambda i: (i, 0)),
        out_shape=jax.ShapeDtypeStruct((m, d), out_dtype),
        compiler_params=_params("parallel"),
        name="rmsnorm",
    )(x2, g.reshape(1, d))


def _inproj_kernel(h_ref, w_ref, o_ref, *, gate, chunk):
    for c in range(o_ref.shape[1] // chunk):
        cs = slice(c * chunk, (c + 1) * chunk)
        acc = jnp.dot(h_ref[...], w_ref[:, cs], preferred_element_type=F32)
        o_ref[:, cs] = (jax.nn.sigmoid(acc) if gate else acc).astype(o_ref.dtype)


def _col_tile(*aligned):
    return 1024 if all(v % 1024 == 0 for v in aligned) else 512


def _inproj(h, w, layer, col0, n, gate):
    m, d = h.shape
    tm = min(1024, m)
    tn = _col_tile(col0, n)
    assert n % tn == 0 and col0 % tn == 0 and col0 + n <= w.shape[2]
    first = col0 // tn
    return pl.pallas_call(
        functools.partial(_inproj_kernel, gate=gate, chunk=MXU_COLS),
        grid=(m // tm, n // tn),
        in_specs=[pl.BlockSpec((tm, d), lambda i, j: (i, 0)),
                  pl.BlockSpec((None, d, tn), lambda i, j: (layer, 0, first + j))],
        out_specs=pl.BlockSpec((tm, tn), lambda i, j: (i, j)),
        out_shape=jax.ShapeDtypeStruct((m, n), BF16),
        compiler_params=_params("parallel", "parallel"),
        name="inproj_gate" if gate else "inproj",
    )(h, w)


def _fgate_kernel(h_ref, wf_ref, bf_ref, o_ref, carry_ref, *, tile):
    @pl.when(pl.program_id(1) == 0)
    def _():
        carry_ref[...] = jnp.zeros_like(carry_ref)

    f = jnp.dot(h_ref[...], wf_ref[...], preferred_element_type=F32) + bf_ref[...]
    ls = jnp.minimum(f, 0.0) - jnp.log1p(jnp.exp(-jnp.abs(f)))
    row = lax.broadcasted_iota(jnp.int32, (tile, tile), 0)
    col = lax.broadcasted_iota(jnp.int32, (tile, tile), 1)
    tri = jnp.where(row >= col, 1.0, 0.0).astype(BF16)
    hi = ls.astype(BF16)
    r1 = ls - hi.astype(F32)
    mid = r1.astype(BF16)
    lo = (r1 - mid.astype(F32)).astype(BF16)
    c = (jnp.dot(tri, hi, preferred_element_type=F32)
         + jnp.dot(tri, mid, preferred_element_type=F32)
         + jnp.dot(tri, lo, preferred_element_type=F32)) + carry_ref[...]
    carry_ref[...] = c[tile - 1:tile, :]
    o_ref[0] = c.T[0:8, :]


def _fgate(h, wf, bf, dm):
    m, d = h.shape
    tile = 512
    nt = dm.seq // tile
    assert dm.fox_heads <= 8
    return pl.pallas_call(
        functools.partial(_fgate_kernel, tile=tile),
        grid=(dm.batch, nt),
        in_specs=[pl.BlockSpec((tile, d), lambda b, t: (b * nt + t, 0)),
                  pl.BlockSpec((d, LANE), lambda b, t: (0, 0)),
                  pl.BlockSpec((1, LANE), lambda b, t: (0, 0))],
        out_specs=pl.BlockSpec((1, 8, tile), lambda b, t: (b, 0, t)),
        out_shape=jax.ShapeDtypeStruct((dm.batch, 8, dm.seq), F32),
        scratch_shapes=[pltpu.VMEM((1, LANE), F32)],
        compiler_params=_params("parallel", "arbitrary"),
        name="fgate",
    )(h, wf, bf)


def _dilated_kernel(q0, k0, v0, q1, k1, v1, q2, k2, v2, gate_ref, o_ref, *, seq):
    nblk = seq // BAND
    scale = HEAD_DIM ** -0.5
    qi = lax.broadcasted_iota(jnp.int32, (BAND, BAND), 0)
    kc = lax.broadcasted_iota(jnp.int32, (BAND, BAND), 1)

    groups = []
    for refs, (window, dil) in zip(((q0, k0, v0), (q1, k1, v1), (q2, k2, v2)), DILATED_CFG):
        same = ((qi - kc) & (dil - 1)) == 0
        oldest = jnp.where(same & (kc >= qi), 0.0, -jnp.inf)
        middle = jnp.where(same, 0.0, -jnp.inf)
        diag = jnp.where(same & (kc <= qi), 0.0, -jnp.inf)
        groups.append((refs, window // BAND, oldest, middle, diag))

    for n in range(nblk):
        rows = slice(n * BAND, (n + 1) * BAND)
        outs, lses = [], []
        for (q_ref, k_ref, v_ref), wblk, oldest, middle, diag in groups:
            first = max(0, n - wblk)
            tiles = [oldest if m == n - wblk else middle for m in range(first, n)] + [diag]
            bias = tiles[0] if len(tiles) == 1 else jnp.concatenate(tiles, axis=1)
            keys = slice(first * BAND, (n + 1) * BAND)
            s = lax.dot_general(q_ref[rows, :], k_ref[keys, :], (((1,), (1,)), ((), ())),
                                preferred_element_type=F32) * scale + bias
            m = jnp.max(s, axis=-1, keepdims=True)
            p = jnp.exp(s - m)
            l = jnp.sum(p, axis=-1, keepdims=True)
            outs.append(jnp.dot(p.astype(BF16), v_ref[keys, :], preferred_element_type=F32) / l)
            lses.append(m + jnp.log(l))
        mx = jnp.maximum(jnp.maximum(lses[0], lses[1]), lses[2])
        e = [jnp.exp(a - mx) for a in lses]
        o = (e[0] * outs[0] + e[1] * outs[1] + e[2] * outs[2]) / (e[0] + e[1] + e[2])
        o_ref[rows, :] = (o * _silu(gate_ref[rows, :].astype(F32))).astype(o_ref.dtype)


def _dilated(pg_lo, dm):
    seq = dm.seq
    assert seq % BAND == 0 and all(d & (d - 1) == 0 for _, d in DILATED_CFG)

    def col_spec(cb, g):
        return pl.BlockSpec((seq, HEAD_DIM), lambda b, j, cb=cb, g=g: (b, cb + g * dm.hpg + j))

    in_specs = []
    for g in range(len(DILATED_CFG)):
        in_specs += [col_spec(dm.lo_aq, g), col_spec(dm.lo_ak, g), col_spec(dm.lo_av, g)]
    in_specs.append(pl.BlockSpec((seq, HEAD_DIM), lambda b, j: (b, dm.lo_ag + j)))
    return pl.pallas_call(
        functools.partial(_dilated_kernel, seq=seq),
        grid=(dm.batch, dm.hpg),
        in_specs=in_specs,
        out_specs=pl.BlockSpec((seq, HEAD_DIM), lambda b, j: (b, j)),
        out_shape=jax.ShapeDtypeStruct((dm.batch * seq, dm.hpg * HEAD_DIM), BF16),
        compiler_params=_params("parallel", "parallel"),
        name="dilated_attn",
    )(*([pg_lo] * 10))


def _conv_kernel(a_ref, b_ref, ah_ref, bh_ref, gate_ref, cw_ref, cb_ref, lg_ref, lb_ref,
                 pw_ref, pwb_ref, o_ref, ybuf, zbuf, cbuf, *, tile, rows):
    width = ybuf.shape[1]
    first = pl.program_id(1) == 0
    ybuf[CONV_HALO:CONV_HALO + tile, :] = (
        a_ref[...].astype(F32) * jax.nn.sigmoid(b_ref[...].astype(F32)))
    halo = ah_ref[...].astype(F32) * jax.nn.sigmoid(bh_ref[...].astype(F32))
    ybuf[0:CONV_HALO, :] = jnp.where(first, 0.0, halo)

    total = CONV_HALO + tile
    for c in range(width // LANE):
        cs = slice(c * LANE, (c + 1) * LANE)
        for b in range(1, SUBLANE):
            zbuf[b - 1, SUBLANE:total, :] = ybuf[SUBLANE - b:total - b, cs]
        taps = [cw_ref[j:j + 1, cs] for j in range(CONV_K)]
        bias = cb_ref[:, cs]
        for rc in range(tile // rows):
            acc = jnp.zeros((rows, LANE), F32)
            for lag in range(CONV_K):
                a, b = divmod(lag, SUBLANE)
                start = CONV_HALO + rc * rows - SUBLANE * a
                src = (ybuf[start:start + rows, cs] if b == 0
                       else zbuf[b - 1, start:start + rows, :])
                acc = acc + taps[CONV_K - 1 - lag] * src
            cbuf[rc * rows:(rc + 1) * rows, cs] = acc + bias

    y = cbuf[...]
    mu = jnp.mean(y, axis=-1, keepdims=True)
    yc = y - mu
    var = jnp.mean(yc * yc, axis=-1, keepdims=True)
    yn = yc * lax.rsqrt(var + EPS) * lg_ref[...] + lb_ref[...]
    z = _silu(yn).astype(BF16)
    out = jnp.dot(z, pw_ref[...], preferred_element_type=F32) + pwb_ref[...]
    o_ref[...] = (out * _silu(gate_ref[...].astype(F32))).astype(o_ref.dtype)


def _conv(pg_lo, cw, cb, lg, lb, pw, pwb, dm):
    width = dm.w4 * LANE
    tile = 512
    nt = dm.seq // tile
    hb = tile // CONV_HALO

    def halo_rows(b, i):
        return jnp.maximum((b * nt + i) * hb - 1, 0)

    ca, cbb, cg = dm.lo_ba // dm.w4, dm.lo_bb // dm.w4, dm.lo_bg // dm.w4
    vec = pl.BlockSpec((1, width), lambda b, i: (0, 0))
    return pl.pallas_call(
        functools.partial(_conv_kernel, tile=tile, rows=128),
        grid=(dm.batch, nt),
        in_specs=[pl.BlockSpec((tile, width), lambda b, i: (b * nt + i, ca)),
                  pl.BlockSpec((tile, width), lambda b, i: (b * nt + i, cbb)),
                  pl.BlockSpec((CONV_HALO, width), lambda b, i: (halo_rows(b, i), ca)),
                  pl.BlockSpec((CONV_HALO, width), lambda b, i: (halo_rows(b, i), cbb)),
                  pl.BlockSpec((tile, width), lambda b, i: (b * nt + i, cg)),
                  pl.BlockSpec((CONV_K, width), lambda b, i: (0, 0)),
                  vec, vec, vec,
                  pl.BlockSpec((width, width), lambda b, i: (0, 0)),
                  vec],
        out_specs=pl.BlockSpec((tile, width), lambda b, i: (b * nt + i, 0)),
        out_shape=jax.ShapeDtypeStruct((dm.batch * dm.seq, width), BF16),
        scratch_shapes=[pltpu.VMEM((CONV_HALO + tile, width), F32),
                        pltpu.VMEM((SUBLANE - 1, CONV_HALO + tile, LANE), F32),
                        pltpu.VMEM((tile, width), F32)],
        compiler_params=_params("parallel", "parallel"),
        name="conformer_conv",
    )(pg_lo, pg_lo, pg_lo, pg_lo, pg_lo, cw, cb.reshape(1, width), lg.reshape(1, width),
      lb.reshape(1, width), pw, pwb.reshape(1, width))


def _fox_kernel(q_ref, k_ref, v_ref, gate_ref, c_ref, o_ref, *, tq):
    i = pl.program_id(2)
    q = q_ref[...]
    scale2 = HEAD_DIM ** -0.5 * LOG2E

    def scores(j):
        off = pl.multiple_of(j * tq, tq)
        kb = k_ref[pl.ds(off, tq), :]
        s = lax.dot_general(q, kb, (((1,), (1,)), ((), ())), preferred_element_type=F32)
        return s * scale2 + c_ref[0, :, pl.ds(off, tq)] * (-LOG2E), off

    def update(s, off, carry):
        m, l, acc = carry
        m_new = jnp.maximum(m, jnp.max(s, axis=-1, keepdims=True))
        alpha = jnp.exp2(m - m_new)
        p = jnp.exp2(s - m_new)
        l = alpha * l + jnp.sum(p, axis=-1, keepdims=True)
        vb = v_ref[pl.ds(off, tq), :]
        acc = alpha * acc + jnp.dot(p.astype(BF16), vb, preferred_element_type=F32)
        return m_new, l, acc

    def below_diagonal(j, carry):
        s, off = scores(j)
        return update(s, off, carry)

    init = (jnp.full((tq, 1), -jnp.inf, F32), jnp.zeros((tq, 1), F32),
            jnp.zeros((tq, HEAD_DIM), F32))
    carry = lax.fori_loop(0, i, below_diagonal, init)
    s, off = scores(i)
    row = lax.broadcasted_iota(jnp.int32, (tq, tq), 0)
    col = lax.broadcasted_iota(jnp.int32, (tq, tq), 1)
    _, l, acc = update(jnp.where(col <= row, s, -jnp.inf), off, carry)
    o_ref[...] = (acc / l * _silu(gate_ref[...].astype(F32))).astype(o_ref.dtype)


def _fox(pg_lo, pg_hi, c_rows, dm):
    seq, heads = dm.seq, dm.fox_heads
    tq = 512
    nq = seq // tq
    return pl.pallas_call(
        functools.partial(_fox_kernel, tq=tq),
        grid=(dm.batch, heads, nq),
        in_specs=[pl.BlockSpec((tq, HEAD_DIM), lambda b, h, i: (b * nq + i, dm.lo_cq + h)),
                  pl.BlockSpec((seq, HEAD_DIM), lambda b, h, i: (b, dm.lo_ck + h)),
                  pl.BlockSpec((seq, HEAD_DIM), lambda b, h, i: (b, dm.lo_cv + h)),
                  pl.BlockSpec((tq, HEAD_DIM), lambda b, h, i: (b * nq + i, dm.hi_cg + h)),
                  pl.BlockSpec((1, 1, seq), lambda b, h, i: (b * 8 + h, 0, 0))],
        out_specs=pl.BlockSpec((tq, HEAD_DIM), lambda b, h, i: (b * nq + i, h)),
        out_shape=jax.ShapeDtypeStruct((dm.batch * seq, heads * HEAD_DIM), BF16),
        compiler_params=_params("parallel", "parallel", "parallel"),
        name="fox_attn",
    )(pg_lo, pg_lo, pg_lo, pg_hi, c_rows)


def _pool_kernel(x_ref, xh_ref, gate_ref, pw_ref, ps_ref, o_ref, xbuf, *, tile):
    i = pl.program_id(1)
    xbuf[POOL_HALO:POOL_HALO + tile, :] = x_ref[...].astype(F32)
    xbuf[0:POOL_HALO, :] = jnp.where(i == 0, 0.0, xh_ref[...].astype(F32))
    gc = xbuf.shape[1] // len(POOL_SIZES)
    t = i * tile + lax.broadcasted_iota(jnp.int32, (tile, 1), 0)
    for g, p in enumerate(POOL_SIZES):
        cs = slice(g * gc, (g + 1) * gc)
        x = xbuf[POOL_HALO:POOL_HALO + tile, cs]
        win = x
        for k in range(1, p):
            win = win + xbuf[POOL_HALO - k:POOL_HALO - k + tile, cs]
        cnt = jnp.minimum(t + 1, p).astype(F32)
        d = (win / cnt - x).astype(BF16)
        y = jnp.dot(d, pw_ref[g], preferred_element_type=F32) * ps_ref[:, cs]
        o_ref[:, cs] = (y * _silu(gate_ref[:, cs].astype(F32))).astype(o_ref.dtype)


def _pool(pg_hi, pw, ps, dm):
    width = dm.w4 * LANE
    tile = 512
    nt = dm.seq // tile
    hb = tile // POOL_HALO
    gc = width // len(POOL_SIZES)
    ci, cg = dm.hi_di // dm.w4, dm.hi_dg // dm.w4
    return pl.pallas_call(
        functools.partial(_pool_kernel, tile=tile),
        grid=(dm.batch, nt),
        in_specs=[pl.BlockSpec((tile, width), lambda b, i: (b * nt + i, ci)),
                  pl.BlockSpec((POOL_HALO, width),
                               lambda b, i: (jnp.maximum((b * nt + i) * hb - 1, 0), ci)),
                  pl.BlockSpec((tile, width), lambda b, i: (b * nt + i, cg)),
                  pl.BlockSpec((len(POOL_SIZES), gc, gc), lambda b, i: (0, 0, 0)),
                  pl.BlockSpec((1, width), lambda b, i: (0, 0))],
        out_specs=pl.BlockSpec((tile, width), lambda b, i: (b * nt + i, 0)),
        out_shape=jax.ShapeDtypeStruct((dm.batch * dm.seq, width), BF16),
        scratch_shapes=[pltpu.VMEM((POOL_HALO + tile, width), F32)],
        compiler_params=_params("parallel", "parallel"),
        name="pool_mixer",
    )(pg_hi, pg_hi, pg_hi, pw, ps.reshape(1, width))


def _merge_kernel(ga, gb, gc, gd, ya, yb, yc, yd, wa, wb, wc, wd, o_ref):
    for c in range(o_ref.shape[1] // MXU_COLS):
        cs = slice(c * MXU_COLS, (c + 1) * MXU_COLS)
        acc = None
        for g, y, w in ((ga, ya, wa), (gb, yb, wb), (gc, yc, wc), (gd, yd, wd)):
            u = jnp.dot(y[...], w[:, cs], preferred_element_type=F32)
            term = g[:, cs].astype(F32) * u
            acc = term if acc is None else acc + term
        o_ref[:, cs] = acc.astype(o_ref.dtype)


def _merge(gates, ys, ws, dm):
    m = dm.batch * dm.seq
    d = dm.d_model
    tm = min(1024, m)
    tn = 512
    per_gate = d // tn
    in_specs = [pl.BlockSpec((tm, tn), lambda i, j, k=k: (i, k * per_gate + j))
                for k in range(4)]
    in_specs += [pl.BlockSpec((tm, y.shape[1]), lambda i, j: (i, 0)) for y in ys]
    in_specs += [pl.BlockSpec((w.shape[0], tn), lambda i, j: (0, j)) for w in ws]
    return pl.pallas_call(
        _merge_kernel,
        grid=(m // tm, d // tn),
        in_specs=in_specs,
        out_specs=pl.BlockSpec((tm, tn), lambda i, j: (i, j)),
        out_shape=jax.ShapeDtypeStruct((m, d), BF16),
        compiler_params=_params("parallel", "parallel"),
        name="gated_merge",
    )(gates, gates, gates, gates, *ys, *ws)


def _outproj_kernel(m_ref, w_ref, x_ref, o_ref):
    for c in range(o_ref.shape[1] // MXU_COLS):
        cs = slice(c * MXU_COLS, (c + 1) * MXU_COLS)
        o_ref[:, cs] = x_ref[:, cs] + jnp.dot(m_ref[...], w_ref[:, cs],
                                             preferred_element_type=F32)


def _outproj(merged, w_out, x2):
    m, d = x2.shape
    tm = min(1024, m)
    tn = 512
    return pl.pallas_call(
        _outproj_kernel,
        grid=(m // tm, d // tn),
        in_specs=[pl.BlockSpec((tm, d), lambda i, j: (i, 0)),
                  pl.BlockSpec((d, tn), lambda i, j: (0, j)),
                  pl.BlockSpec((tm, tn), lambda i, j: (i, j))],
        out_specs=pl.BlockSpec((tm, tn), lambda i, j: (i, j)),
        out_shape=jax.ShapeDtypeStruct((m, d), F32),
        compiler_params=_params("parallel", "parallel"),
        name="outproj",
    )(merged, w_out, x2)


def kernel(x, norm_g, w_in, b_forget, conv_w, conv_b, conv_ln_g, conv_ln_b, conv_pw, conv_pw_b,
           pool_w, pool_scale, w_branch, w_out, final_g):
    dm = _dims(x.shape)
    depth = norm_g.shape[0]
    x2 = x.reshape(dm.batch * dm.seq, dm.d_model)
    heads = dm.fox_heads
    cf_col = dm.lo_blocks * LANE
    row_off = (0, dm.hpg * LANE, (dm.hpg + dm.w4) * LANE, (dm.hpg + 2 * dm.w4) * LANE,
               (dm.hpg + 3 * dm.w4) * LANE)

    for l in range(depth):
        wi = w_in[l]
        w_lo = wi[:, :cf_col].astype(BF16)[None]
        w_hi = wi[:, cf_col + heads:].astype(BF16)[None]
        wf = jnp.pad(wi[:, cf_col:cf_col + heads], ((0, 0), (0, LANE - heads))).astype(BF16)
        bf = jnp.pad(b_forget[l], (0, LANE - heads)).reshape(1, LANE)
        ws = [w_branch[l, row_off[k]:row_off[k + 1]].astype(BF16) for k in range(4)]

        h = _rmsnorm(x2, norm_g[l], BF16)
        pg_lo = _inproj(h, w_lo, 0, 0, cf_col, False)
        pg_hi = _inproj(h, w_hi, 0, 0, dm.hi_gate * LANE, False)
        gates = _inproj(h, w_hi, 0, dm.hi_gate * LANE, 4 * dm.d_model, True)
        c_rows = _fgate(h, wf, bf, dm).reshape(dm.batch * 8, 1, dm.seq)
        y_a = _dilated(pg_lo, dm)
        y_b = _conv(pg_lo, conv_w[l], conv_b[l], conv_ln_g[l], conv_ln_b[l],
                    conv_pw[l].astype(BF16), conv_pw_b[l], dm)
        y_c = _fox(pg_lo, pg_hi, c_rows, dm)
        y_d = _pool(pg_hi, pool_w[l].astype(BF16), pool_scale[l], dm)
        merged = _merge(gates, (y_a, y_b, y_c, y_d), ws, dm)
        x2 = _outproj(merged, w_out[l].astype(BF16), x2)

    return _rmsnorm(x2, final_g, x.dtype).reshape(x.shape)
```

```python
import functools
import math
from typing import NamedTuple

import jax
import jax.numpy as jnp
from jax import lax
from jax.experimental import pallas as pl
from jax.experimental.pallas import tpu as pltpu

F32 = jnp.float32
BF16 = jnp.bfloat16

EPS = 1e-6
HEAD_DIM = 128
LANE = 128
SUBLANE = 8
MXU_COLS = 256
DILATED_CFG = ((128, 1), (512, 4), (2048, 16))
BAND = 128
CONV_K = 31
CONV_HALO = 32
POOL_SIZES = (2, 4, 8, 16)
POOL_HALO = 16
VMEM_LIMIT = 56 * 1024 * 1024
LOG2E = math.log2(math.e)


class _Dims(NamedTuple):
    batch: int
    seq: int
    d_model: int
    hpg: int
    w4: int
    fox_heads: int
    lo_aq: int
    lo_ak: int
    lo_av: int
    lo_ag: int
    lo_ba: int
    lo_bb: int
    lo_bg: int
    lo_cq: int
    lo_ck: int
    lo_cv: int
    lo_blocks: int
    hi_cg: int
    hi_di: int
    hi_dg: int
    hi_gate: int
    hi_blocks: int


def _dims(x_shape):
    batch, seq, d = x_shape
    hpg = d // 1024
    w4 = d // 512
    lo_ba = 10 * hpg
    lo_cq = lo_ba + 3 * w4
    return _Dims(batch=batch, seq=seq, d_model=d, hpg=hpg, w4=w4, fox_heads=d // 512,
                 lo_aq=0, lo_ak=3 * hpg, lo_av=6 * hpg, lo_ag=9 * hpg,
                 lo_ba=lo_ba, lo_bb=lo_ba + w4, lo_bg=lo_ba + 2 * w4,
                 lo_cq=lo_cq, lo_ck=lo_cq + w4, lo_cv=lo_cq + 2 * w4, lo_blocks=lo_cq + 3 * w4,
                 hi_cg=0, hi_di=w4, hi_dg=2 * w4, hi_gate=3 * w4,
                 hi_blocks=3 * w4 + 4 * (d // LANE))


def _params(*sem):
    return pltpu.CompilerParams(dimension_semantics=sem, vmem_limit_bytes=VMEM_LIMIT)


def _silu(x):
    return x * jax.nn.sigmoid(x)


def _rmsnorm_kernel(x_ref, g_ref, o_ref):
    x = x_ref[...]
    ms = jnp.mean(x * x, axis=-1, keepdims=True)
    o_ref[...] = (x * lax.rsqrt(ms + EPS) * g_ref[...]).astype(o_ref.dtype)


def _rmsnorm(x2, g, out_dtype):
    m, d = x2.shape
    tm = min(512, m)
    return pl.pallas_call(
        _rmsnorm_kernel,
        grid=(m // tm,),
        in_specs=[pl.BlockSpec((tm, d), lambda i: (i, 0)),
                  pl.BlockSpec((1, d), lambda i: (0, 0))],
        out_specs=pl.BlockSpec((tm, d), lambda i: (i, 0)),
        out_shape=jax.ShapeDtypeStruct((m, d), out_dtype),
        compiler_params=_params("parallel"),
        name="rmsnorm",
    )(x2, g.reshape(1, d))


def _inproj_kernel(h_ref, w_ref, o_ref, *, gate, chunk):
    for c in range(o_ref.shape[1] // chunk):
        cs = slice(c * chunk, (c + 1) * chunk)
        acc = jnp.dot(h_ref[...], w_ref[:, cs], preferred_element_type=F32)
        o_ref[:, cs] = (0.5 * jnp.tanh(0.5 * acc) + 0.5 if gate else acc).astype(o_ref.dtype)


def _col_tile(*aligned):
    return 1024 if all(v % 1024 == 0 for v in aligned) else 512


def _inproj(h, w, layer, col0, n, gate):
    m, d = h.shape
    tm = min(1024, m)
    tn = _col_tile(col0, n)
    assert n % tn == 0 and col0 % tn == 0 and col0 + n <= w.shape[2]
    first = col0 // tn
    return pl.pallas_call(
        functools.partial(_inproj_kernel, gate=gate, chunk=MXU_COLS),
        grid=(m // tm, n // tn),
        in_specs=[pl.BlockSpec((tm, d), lambda i, j: (i, 0)),
                  pl.BlockSpec((None, d, tn), lambda i, j: (layer, 0, first + j))],
        out_specs=pl.BlockSpec((tm, tn), lambda i, j: (i, j)),
        out_shape=jax.ShapeDtypeStruct((m, n), BF16),
        compiler_params=_params("parallel", "parallel"),
        name="inproj_gate" if gate else "inproj",
    )(h, w)


def _fgate_kernel(h_ref, wf_ref, bf_ref, o_ref, carry_ref, *, tile):
    @pl.when(pl.program_id(1) == 0)
    def _():
        carry_ref[...] = jnp.zeros_like(carry_ref)

    f = jnp.dot(h_ref[...], wf_ref[...], preferred_element_type=F32) + bf_ref[...]
    ls = jnp.minimum(f, 0.0) - jnp.log1p(jnp.exp(-jnp.abs(f)))
    row = lax.broadcasted_iota(jnp.int32, (tile, tile), 0)
    col = lax.broadcasted_iota(jnp.int32, (tile, tile), 1)
    tri = jnp.where(row >= col, 1.0, 0.0).astype(BF16)
    hi = ls.astype(BF16)
    r1 = ls - hi.astype(F32)
    mid = r1.astype(BF16)
    lo = (r1 - mid.astype(F32)).astype(BF16)
    c = (jnp.dot(tri, hi, preferred_element_type=F32)
         + jnp.dot(tri, mid, preferred_element_type=F32)
         + jnp.dot(tri, lo, preferred_element_type=F32)) + carry_ref[...]
    carry_ref[...] = c[tile - 1:tile, :]
    o_ref[0] = c.T[0:8, :]


def _fgate(h, wf, bf, dm):
    m, d = h.shape
    tile = 512
    nt = dm.seq // tile
    assert dm.fox_heads <= 8
    return pl.pallas_call(
        functools.partial(_fgate_kernel, tile=tile),
        grid=(dm.batch, nt),
        in_specs=[pl.BlockSpec((tile, d), lambda b, t: (b * nt + t, 0)),
                  pl.BlockSpec((d, LANE), lambda b, t: (0, 0)),
                  pl.BlockSpec((1, LANE), lambda b, t: (0, 0))],
        out_specs=pl.BlockSpec((1, 8, tile), lambda b, t: (b, 0, t)),
        out_shape=jax.ShapeDtypeStruct((dm.batch, 8, dm.seq), F32),
        scratch_shapes=[pltpu.VMEM((1, LANE), F32)],
        compiler_params=_params("parallel", "arbitrary"),
        name="fgate",
    )(h, wf, bf)


RESIDUE_STAGES = (4, 4)


def _residue_major(ref, src, tmp, dst):
    s1, s2 = RESIDUE_STAGES
    part = ref.shape[0] // s1
    sub = part // s2
    src[...] = ref[...].astype(F32)
    for r1 in range(s1):
        tmp[r1 * part:(r1 + 1) * part, :] = src[pl.ds(r1, part, stride=s1), :]
    for r1 in range(s1):
        for r2 in range(s2):
            dst[r2 * s1 + r1] = tmp[pl.ds(r1 * part + r2, sub, stride=s2), :].astype(dst.dtype)


def _token_order(val, tmp, dst):
    s1, s2 = RESIDUE_STAGES
    part = dst.shape[0] // s1
    sub = part // s2
    for r1 in range(s1):
        for r2 in range(s2):
            tmp[pl.ds(r1 * part + r2, sub, stride=s2), :] = val[r2 * s1 + r1]
    for r1 in range(s1):
        dst[pl.ds(r1, part, stride=s1), :] = tmp[r1 * part:(r1 + 1) * part, :]


def _dilated_kernel(q0, k0, v0, q1, k1, v1, q2, k2, v2, gate_ref, o_ref,
                    src, tmp, q_rm, k_rm, v_rm, o_tok, l_tok, *, seq):
    nblk = seq // BAND
    scale = HEAD_DIM ** -0.5
    qi = lax.broadcasted_iota(jnp.int32, (BAND, BAND), 0)
    kc = lax.broadcasted_iota(jnp.int32, (BAND, BAND), 1)

    groups = []
    for refs, (window, dil) in zip(((q0, k0, v0), (q1, k1, v1), (q2, k2, v2)), DILATED_CFG):
        if dil * BAND == seq and dil == RESIDUE_STAGES[0] * RESIDUE_STAGES[1]:
            for ref, dst in zip(refs, (q_rm, k_rm, v_rm)):
                _residue_major(ref, src, tmp, dst)
            s = jnp.einsum('rqd,rkd->rqk', q_rm[...], k_rm[...],
                           preferred_element_type=F32) * scale
            s = s + jnp.where(kc <= qi, 0.0, -jnp.inf)[None]
            m = jnp.max(s, axis=-1, keepdims=True)
            p = jnp.exp(s - m)
            l = jnp.sum(p, axis=-1, keepdims=True)
            o = jnp.einsum('rqk,rkd->rqd', p.astype(BF16), v_rm[...],
                           preferred_element_type=F32) / l
            _token_order(o, tmp, o_tok)
            _token_order(jnp.broadcast_to(m + jnp.log(l), o.shape), tmp, l_tok)
            groups.append(None)
            continue
        same = ((qi - kc) & (dil - 1)) == 0
        oldest = jnp.where(same & (kc >= qi), 0.0, -jnp.inf)
        middle = jnp.where(same, 0.0, -jnp.inf)
        diag = jnp.where(same & (kc <= qi), 0.0, -jnp.inf)
        groups.append((refs, window // BAND, oldest, middle, diag))

    for n in range(nblk):
        rows = slice(n * BAND, (n + 1) * BAND)
        outs, lses = [], []
        for group in groups:
            if group is None:
                outs.append(o_tok[rows, :])
                lses.append(l_tok[rows, :])
                continue
            (q_ref, k_ref, v_ref), wblk, oldest, middle, diag = group
            first = max(0, n - wblk)
            tiles = [oldest if m == n - wblk else middle for m in range(first, n)] + [diag]
            bias = tiles[0] if len(tiles) == 1 else jnp.concatenate(tiles, axis=1)
            keys = slice(first * BAND, (n + 1) * BAND)
            s = lax.dot_general(q_ref[rows, :], k_ref[keys, :], (((1,), (1,)), ((), ())),
                                preferred_element_type=F32) * scale + bias
            m = jnp.max(s, axis=-1, keepdims=True)
            p = jnp.exp(s - m)
            l = jnp.sum(p, axis=-1, keepdims=True)
            outs.append(jnp.dot(p.astype(BF16), v_ref[keys, :], preferred_element_type=F32) / l)
            lses.append(m + jnp.log(l))
        mx = jnp.maximum(jnp.maximum(lses[0], lses[1]), lses[2])
        e = [jnp.exp(a - mx) for a in lses]
        o = (e[0] * outs[0] + e[1] * outs[1] + e[2] * outs[2]) / (e[0] + e[1] + e[2])
        o_ref[rows, :] = (o * _silu(gate_ref[rows, :].astype(F32))).astype(o_ref.dtype)


def _dilated(pg_lo, dm):
    seq = dm.seq
    assert seq % BAND == 0 and all(d & (d - 1) == 0 for _, d in DILATED_CFG)

    def col_spec(cb, g):
        return pl.BlockSpec((seq, HEAD_DIM), lambda b, j, cb=cb, g=g: (b, cb + g * dm.hpg + j))

    in_specs = []
    for g in range(len(DILATED_CFG)):
        in_specs += [col_spec(dm.lo_aq, g), col_spec(dm.lo_ak, g), col_spec(dm.lo_av, g)]
    in_specs.append(pl.BlockSpec((seq, HEAD_DIM), lambda b, j: (b, dm.lo_ag + j)))
    tok = pltpu.VMEM((seq, HEAD_DIM), F32)
    res = pltpu.VMEM((seq // BAND, BAND, HEAD_DIM), BF16)
    return pl.pallas_call(
        functools.partial(_dilated_kernel, seq=seq),
        grid=(dm.batch, dm.hpg),
        in_specs=in_specs,
        out_specs=pl.BlockSpec((seq, HEAD_DIM), lambda b, j: (b, j)),
        out_shape=jax.ShapeDtypeStruct((dm.batch * seq, dm.hpg * HEAD_DIM), BF16),
        scratch_shapes=[tok, tok, res, res, res, tok, tok],
        compiler_params=_params("parallel", "parallel"),
        name="dilated_attn",
    )(*([pg_lo] * 10))


def _conv_kernel(a_ref, b_ref, ah_ref, bh_ref, gate_ref, cw_ref, cb_ref, lg_ref, lb_ref,
                 pw_ref, pwb_ref, o_ref, ybuf, zbuf, cbuf, *, tile, rows):
    width = ybuf.shape[1]
    first = pl.program_id(1) == 0
    ybuf[CONV_HALO:CONV_HALO + tile, :] = (
        a_ref[...].astype(F32) * jax.nn.sigmoid(b_ref[...].astype(F32)))
    halo = ah_ref[...].astype(F32) * jax.nn.sigmoid(bh_ref[...].astype(F32))
    ybuf[0:CONV_HALO, :] = jnp.where(first, 0.0, halo)

    total = CONV_HALO + tile
    for c in range(width // LANE):
        cs = slice(c * LANE, (c + 1) * LANE)
        for b in range(1, SUBLANE):
            zbuf[b - 1, SUBLANE:total, :] = ybuf[SUBLANE - b:total - b, cs]
        taps = [cw_ref[j:j + 1, cs] for j in range(CONV_K)]
        bias = cb_ref[:, cs]
        for rc in range(tile // rows):
            acc = jnp.zeros((rows, LANE), F32)
            for lag in range(CONV_K):
                a, b = divmod(lag, SUBLANE)
                start = CONV_HALO + rc * rows - SUBLANE * a
                src = (ybuf[start:start + rows, cs] if b == 0
                       else zbuf[b - 1, start:start + rows, :])
                acc = acc + taps[CONV_K - 1 - lag] * src
            cbuf[rc * rows:(rc + 1) * rows, cs] = acc + bias

    y = cbuf[...]
    mu = jnp.mean(y, axis=-1, keepdims=True)
    yc = y - mu
    var = jnp.mean(yc * yc, axis=-1, keepdims=True)
    yn = yc * lax.rsqrt(var + EPS) * lg_ref[...] + lb_ref[...]
    z = _silu(yn).astype(BF16)
    out = jnp.dot(z, pw_ref[...], preferred_element_type=F32) + pwb_ref[...]
    o_ref[...] = (out * _silu(gate_ref[...].astype(F32))).astype(o_ref.dtype)


def _conv(pg_lo, cw, cb, lg, lb, pw, pwb, dm):
    width = dm.w4 * LANE
    tile = 512
    nt = dm.seq // tile
    hb = tile // CONV_HALO

    def halo_rows(b, i):
        return jnp.maximum((b * nt + i) * hb - 1, 0)

    ca, cbb, cg = dm.lo_ba // dm.w4, dm.lo_bb // dm.w4, dm.lo_bg // dm.w4
    vec = pl.BlockSpec((1, width), lambda b, i: (0, 0))
    return pl.pallas_call(
        functools.partial(_conv_kernel, tile=tile, rows=128),
        grid=(dm.batch, nt),
        in_specs=[pl.BlockSpec((tile, width), lambda b, i: (b * nt + i, ca)),
                  pl.BlockSpec((tile, width), lambda b, i: (b * nt + i, cbb)),
                  pl.BlockSpec((CONV_HALO, width), lambda b, i: (halo_rows(b, i), ca)),
                  pl.BlockSpec((CONV_HALO, width), lambda b, i: (halo_rows(b, i), cbb)),
                  pl.BlockSpec((tile, width), lambda b, i: (b * nt + i, cg)),
                  pl.BlockSpec((CONV_K, width), lambda b, i: (0, 0)),
                  vec, vec, vec,
                  pl.BlockSpec((width, width), lambda b, i: (0, 0)),
                  vec],
        out_specs=pl.BlockSpec((tile, width), lambda b, i: (b * nt + i, 0)),
        out_shape=jax.ShapeDtypeStruct((dm.batch * dm.seq, width), BF16),
        scratch_shapes=[pltpu.VMEM((CONV_HALO + tile, width), F32),
                        pltpu.VMEM((SUBLANE - 1, CONV_HALO + tile, LANE), F32),
                        pltpu.VMEM((tile, width), F32)],
        compiler_params=_params("parallel", "parallel"),
        name="conformer_conv",
    )(pg_lo, pg_lo, pg_lo, pg_lo, pg_lo, cw, cb.reshape(1, width), lg.reshape(1, width),
      lb.reshape(1, width), pw, pwb.reshape(1, width))


def _fox_kernel(q_ref, k_ref, v_ref, gate_ref, c_ref, o_ref, *, tq):
    i = pl.program_id(2)
    q = q_ref[...]
    scale2 = HEAD_DIM ** -0.5 * LOG2E

    def scores(j):
        off = pl.multiple_of(j * tq, tq)
        kb = k_ref[pl.ds(off, tq), :]
        s = lax.dot_general(q, kb, (((1,), (1,)), ((), ())), preferred_element_type=F32)
        return s * scale2 + c_ref[0, :, pl.ds(off, tq)] * (-LOG2E), off

    def update(s, off, carry):
        m, l, acc = carry
        m_new = jnp.maximum(m, jnp.max(s, axis=-1, keepdims=True))
        alpha = jnp.exp2(m - m_new)
        p = jnp.exp2(s - m_new)
        l = alpha * l + jnp.sum(p, axis=-1, keepdims=True)
        vb = v_ref[pl.ds(off, tq), :]
        acc = alpha * acc + jnp.dot(p.astype(BF16), vb, preferred_element_type=F32)
        return m_new, l, acc

    def below_diagonal(j, carry):
        s, off = scores(j)
        return update(s, off, carry)

    init = (jnp.full((tq, 1), -jnp.inf, F32), jnp.zeros((tq, 1), F32),
            jnp.zeros((tq, HEAD_DIM), F32))
    carry = lax.fori_loop(0, i, below_diagonal, init)
    s, off = scores(i)
    row = lax.broadcasted_iota(jnp.int32, (tq, tq), 0)
    col = lax.broadcasted_iota(jnp.int32, (tq, tq), 1)
    _, l, acc = update(jnp.where(col <= row, s, -jnp.inf), off, carry)
    o_ref[...] = (acc / l * _silu(gate_ref[...].astype(F32))).astype(o_ref.dtype)


def _fox(pg_lo, pg_hi, c_rows, dm):
    seq, heads = dm.seq, dm.fox_heads
    tq = 512
    nq = seq // tq
    return pl.pallas_call(
        functools.partial(_fox_kernel, tq=tq),
        grid=(dm.batch, heads, nq),
        in_specs=[pl.BlockSpec((tq, HEAD_DIM), lambda b, h, i: (b * nq + i, dm.lo_cq + h)),
                  pl.BlockSpec((seq, HEAD_DIM), lambda b, h, i: (b, dm.lo_ck + h)),
                  pl.BlockSpec((seq, HEAD_DIM), lambda b, h, i: (b, dm.lo_cv + h)),
                  pl.BlockSpec((tq, HEAD_DIM), lambda b, h, i: (b * nq + i, dm.hi_cg + h)),
                  pl.BlockSpec((1, 1, seq), lambda b, h, i: (b * 8 + h, 0, 0))],
        out_specs=pl.BlockSpec((tq, HEAD_DIM), lambda b, h, i: (b * nq + i, h)),
        out_shape=jax.ShapeDtypeStruct((dm.batch * seq, heads * HEAD_DIM), BF16),
        compiler_params=_params("parallel", "parallel", "parallel"),
        name="fox_attn",
    )(pg_lo, pg_lo, pg_lo, pg_hi, c_rows)


def _pool_kernel(x_ref, xh_ref, gate_ref, pw_ref, ps_ref, o_ref, xbuf, *, tile):
    i = pl.program_id(1)
    xbuf[POOL_HALO:POOL_HALO + tile, :] = x_ref[...].astype(F32)
    xbuf[0:POOL_HALO, :] = jnp.where(i == 0, 0.0, xh_ref[...].astype(F32))
    gc = xbuf.shape[1] // len(POOL_SIZES)
    t = i * tile + lax.broadcasted_iota(jnp.int32, (tile, 1), 0)
    for g, p in enumerate(POOL_SIZES):
        cs = slice(g * gc, (g + 1) * gc)
        x = xbuf[POOL_HALO:POOL_HALO + tile, cs]
        win = x
        for k in range(1, p):
            win = win + xbuf[POOL_HALO - k:POOL_HALO - k + tile, cs]
        cnt = jnp.minimum(t + 1, p).astype(F32)
        d = (win / cnt - x).astype(BF16)
        y = jnp.dot(d, pw_ref[g], preferred_element_type=F32) * ps_ref[:, cs]
        o_ref[:, cs] = (y * _silu(gate_ref[:, cs].astype(F32))).astype(o_ref.dtype)


def _pool(pg_hi, pw, ps, dm):
    width = dm.w4 * LANE
    tile = 512
    nt = dm.seq // tile
    hb = tile // POOL_HALO
    gc = width // len(POOL_SIZES)
    ci, cg = dm.hi_di // dm.w4, dm.hi_dg // dm.w4
    return pl.pallas_call(
        functools.partial(_pool_kernel, tile=tile),
        grid=(dm.batch, nt),
        in_specs=[pl.BlockSpec((tile, width), lambda b, i: (b * nt + i, ci)),
                  pl.BlockSpec((POOL_HALO, width),
                               lambda b, i: (jnp.maximum((b * nt + i) * hb - 1, 0), ci)),
                  pl.BlockSpec((tile, width), lambda b, i: (b * nt + i, cg)),
                  pl.BlockSpec((len(POOL_SIZES), gc, gc), lambda b, i: (0, 0, 0)),
                  pl.BlockSpec((1, width), lambda b, i: (0, 0))],
        out_specs=pl.BlockSpec((tile, width), lambda b, i: (b * nt + i, 0)),
        out_shape=jax.ShapeDtypeStruct((dm.batch * dm.seq, width), BF16),
        scratch_shapes=[pltpu.VMEM((POOL_HALO + tile, width), F32)],
        compiler_params=_params("parallel", "parallel"),
        name="pool_mixer",
    )(pg_hi, pg_hi, pg_hi, pw, ps.reshape(1, width))


def _merge_kernel(ga, gb, gc, gd, ya, yb, yc, yd, wa, wb, wc, wd, o_ref):
    for c in range(o_ref.shape[1] // MXU_COLS):
        cs = slice(c * MXU_COLS, (c + 1) * MXU_COLS)
        acc = None
        for g, y, w in ((ga, ya, wa), (gb, yb, wb), (gc, yc, wc), (gd, yd, wd)):
            u = jnp.dot(y[...], w[:, cs], preferred_element_type=F32)
            term = g[:, cs].astype(F32) * u
            acc = term if acc is None else acc + term
        o_ref[:, cs] = acc.astype(o_ref.dtype)


def _merge(gates, ys, ws, dm):
    m = dm.batch * dm.seq
    d = dm.d_model
    tm = min(1024, m)
    tn = 512
    per_gate = d // tn
    in_specs = [pl.BlockSpec((tm, tn), lambda i, j, k=k: (i, k * per_gate + j))
                for k in range(4)]
    in_specs += [pl.BlockSpec((tm, y.shape[1]), lambda i, j: (i, 0)) for y in ys]
    in_specs += [pl.BlockSpec((w.shape[0], tn), lambda i, j: (0, j)) for w in ws]
    return pl.pallas_call(
        _merge_kernel,
        grid=(m // tm, d // tn),
        in_specs=in_specs,
        out_specs=pl.BlockSpec((tm, tn), lambda i, j: (i, j)),
        out_shape=jax.ShapeDtypeStruct((m, d), BF16),
        compiler_params=_params("parallel", "parallel"),
        name="gated_merge",
    )(gates, gates, gates, gates, *ys, *ws)


def _outproj_kernel(m_ref, w_ref, x_ref, o_ref):
    for c in range(o_ref.shape[1] // MXU_COLS):
        cs = slice(c * MXU_COLS, (c + 1) * MXU_COLS)
        o_ref[:, cs] = x_ref[:, cs] + jnp.dot(m_ref[...], w_ref[:, cs],
                                             preferred_element_type=F32)


def _outproj(merged, w_out, x2):
    m, d = x2.shape
    tm = min(1024, m)
    tn = 512
    return pl.pallas_call(
        _outproj_kernel,
        grid=(m // tm, d // tn),
        in_specs=[pl.BlockSpec((tm, d), lambda i, j: (i, 0)),
                  pl.BlockSpec((d, tn), lambda i, j: (0, j)),
                  pl.BlockSpec((tm, tn), lambda i, j: (i, j))],
        out_specs=pl.BlockSpec((tm, tn), lambda i, j: (i, j)),
        out_shape=jax.ShapeDtypeStruct((m, d), F32),
        compiler_params=_params("parallel", "parallel"),
        name="outproj",
    )(merged, w_out, x2)


def kernel(x, norm_g, w_in, b_forget, conv_w, conv_b, conv_ln_g, conv_ln_b, conv_pw, conv_pw_b,
           pool_w, pool_scale, w_branch, w_out, final_g):
    dm = _dims(x.shape)
    depth = norm_g.shape[0]
    x2 = x.reshape(dm.batch * dm.seq, dm.d_model)
    heads = dm.fox_heads
    cf_col = dm.lo_blocks * LANE
    row_off = (0, dm.hpg * LANE, (dm.hpg + dm.w4) * LANE, (dm.hpg + 2 * dm.w4) * LANE,
               (dm.hpg + 3 * dm.w4) * LANE)

    w_bf = w_in.astype(BF16)
    w_hi = w_bf[:, :, cf_col + heads:]
    for l in range(depth):
        wf = jnp.pad(w_bf[l, :, cf_col:cf_col + heads], ((0, 0), (0, LANE - heads)))
        bf = jnp.pad(b_forget[l], (0, LANE - heads)).reshape(1, LANE)
        ws = [w_branch[l, row_off[k]:row_off[k + 1]].astype(BF16) for k in range(4)]

        h = _rmsnorm(x2, norm_g[l], BF16)
        pg_lo = _inproj(h, w_bf, l, 0, cf_col, False)
        pg_hi = _inproj(h, w_hi, l, 0, dm.hi_gate * LANE, False)
        gates = _inproj(h, w_hi, l, dm.hi_gate * LANE, 4 * dm.d_model, True)
        c_rows = _fgate(h, wf, bf, dm).reshape(dm.batch * 8, 1, dm.seq)
        y_a = _dilated(pg_lo, dm)
        y_b = _conv(pg_lo, conv_w[l], conv_b[l], conv_ln_g[l], conv_ln_b[l],
                    conv_pw[l].astype(BF16), conv_pw_b[l], dm)
        y_c = _fox(pg_lo, pg_hi, c_rows, dm)
        y_d = _pool(pg_hi, pool_w[l].astype(BF16), pool_scale[l], dm)
        merged = _merge(gates, (y_a, y_b, y_c, y_d), ws, dm)
        x2 = _outproj(merged, w_out[l].astype(BF16), x2)

    return _rmsnorm(x2, final_g, x.dtype).reshape(x.shape)
```

```python
import functools
import math
from typing import NamedTuple

import jax
import jax.numpy as jnp
from jax import lax
from jax.experimental import pallas as pl
from jax.experimental.pallas import tpu as pltpu

F32 = jnp.float32
BF16 = jnp.bfloat16

EPS = 1e-6
HEAD_DIM = 128
LANE = 128
SUBLANE = 8
MXU_COLS = 256
DILATED_CFG = ((128, 1), (512, 4), (2048, 16))
BAND = 128
CONV_K = 31
CONV_HALO = 32
POOL_SIZES = (2, 4, 8, 16)
POOL_HALO = 16
VMEM_LIMIT = 56 * 1024 * 1024
LOG2E = math.log2(math.e)


class _Dims(NamedTuple):
    batch: int
    seq: int
    d_model: int
    hpg: int
    w4: int
    fox_heads: int
    lo_aq: int
    lo_ak: int
    lo_av: int
    lo_ag: int
    lo_ba: int
    lo_bb: int
    lo_bg: int
    lo_cq: int
    lo_ck: int
    lo_cv: int
    lo_blocks: int
    hi_cg: int
    hi_di: int
    hi_dg: int
    hi_gate: int
    hi_blocks: int


def _dims(x_shape):
    batch, seq, d = x_shape
    hpg = d // 1024
    w4 = d // 512
    lo_ba = 10 * hpg
    lo_cq = lo_ba + 3 * w4
    return _Dims(batch=batch, seq=seq, d_model=d, hpg=hpg, w4=w4, fox_heads=d // 512,
                 lo_aq=0, lo_ak=3 * hpg, lo_av=6 * hpg, lo_ag=9 * hpg,
                 lo_ba=lo_ba, lo_bb=lo_ba + w4, lo_bg=lo_ba + 2 * w4,
                 lo_cq=lo_cq, lo_ck=lo_cq + w4, lo_cv=lo_cq + 2 * w4, lo_blocks=lo_cq + 3 * w4,
                 hi_cg=0, hi_di=w4, hi_dg=2 * w4, hi_gate=3 * w4,
                 hi_blocks=3 * w4 + 4 * (d // LANE))


def _params(*sem):
    return pltpu.CompilerParams(dimension_semantics=sem, vmem_limit_bytes=VMEM_LIMIT)


def _silu(x):
    return x * jax.nn.sigmoid(x)


def _rmsnorm_kernel(x_ref, g_ref, o_ref):
    x = x_ref[...]
    ms = jnp.mean(x * x, axis=-1, keepdims=True)
    o_ref[...] = (x * lax.rsqrt(ms + EPS) * g_ref[...]).astype(o_ref.dtype)


def _rmsnorm(x2, g, out_dtype):
    m, d = x2.shape
    tm = min(512, m)
    return pl.pallas_call(
        _rmsnorm_kernel,
        grid=(m // tm,),
        in_specs=[pl.BlockSpec((tm, d), lambda i: (i, 0)),
                  pl.BlockSpec((1, d), lambda i: (0, 0))],
        out_specs=pl.BlockSpec((tm, d), lambda i: (i, 0)),
        out_shape=jax.ShapeDtypeStruct((m, d), out_dtype),
        compiler_params=_params("parallel"),
        name="rmsnorm",
    )(x2, g.reshape(1, d))


def _inproj_kernel(h_ref, w_ref, o_ref, *, gate, chunk):
    for c in range(o_ref.shape[1] // chunk):
        cs = slice(c * chunk, (c + 1) * chunk)
        acc = jnp.dot(h_ref[...], w_ref[:, cs], preferred_element_type=F32)
        o_ref[:, cs] = (0.5 * jnp.tanh(0.5 * acc) + 0.5 if gate else acc).astype(o_ref.dtype)


def _col_tile(*aligned):
    return 1024 if all(v % 1024 == 0 for v in aligned) else 512


def _inproj(h, w, layer, col0, n, gate):
    m, d = h.shape
    tm = min(1024, m)
    tn = _col_tile(col0, n)
    assert n % tn == 0 and col0 % tn == 0 and col0 + n <= w.shape[2]
    first = col0 // tn
    return pl.pallas_call(
        functools.partial(_inproj_kernel, gate=gate, chunk=MXU_COLS),
        grid=(m // tm, n // tn),
        in_specs=[pl.BlockSpec((tm, d), lambda i, j: (i, 0)),
                  pl.BlockSpec((None, d, tn), lambda i, j: (layer, 0, first + j))],
        out_specs=pl.BlockSpec((tm, tn), lambda i, j: (i, j)),
        out_shape=jax.ShapeDtypeStruct((m, n), BF16),
        compiler_params=_params("parallel", "parallel"),
        name="inproj_gate" if gate else "inproj",
    )(h, w)


def _fgate_kernel(h_ref, wf_ref, bf_ref, o_ref, carry_ref, *, tile):
    @pl.when(pl.program_id(1) == 0)
    def _():
        carry_ref[...] = jnp.zeros_like(carry_ref)

    f = jnp.dot(h_ref[...], wf_ref[...], preferred_element_type=F32) + bf_ref[...]
    ls = jnp.minimum(f, 0.0) - jnp.log1p(jnp.exp(-jnp.abs(f)))
    row = lax.broadcasted_iota(jnp.int32, (tile, tile), 0)
    col = lax.broadcasted_iota(jnp.int32, (tile, tile), 1)
    tri = jnp.where(row >= col, 1.0, 0.0).astype(BF16)
    hi = ls.astype(BF16)
    r1 = ls - hi.astype(F32)
    mid = r1.astype(BF16)
    lo = (r1 - mid.astype(F32)).astype(BF16)
    c = (jnp.dot(tri, hi, preferred_element_type=F32)
         + jnp.dot(tri, mid, preferred_element_type=F32)
         + jnp.dot(tri, lo, preferred_element_type=F32)) + carry_ref[...]
    carry_ref[...] = c[tile - 1:tile, :]
    o_ref[0] = c.T[0:8, :]


def _fgate(h, wf, bf, dm):
    m, d = h.shape
    tile = 512
    nt = dm.seq // tile
    assert dm.fox_heads <= 8
    return pl.pallas_call(
        functools.partial(_fgate_kernel, tile=tile),
        grid=(dm.batch, nt),
        in_specs=[pl.BlockSpec((tile, d), lambda b, t: (b * nt + t, 0)),
                  pl.BlockSpec((d, LANE), lambda b, t: (0, 0)),
                  pl.BlockSpec((1, LANE), lambda b, t: (0, 0))],
        out_specs=pl.BlockSpec((1, 8, tile), lambda b, t: (b, 0, t)),
        out_shape=jax.ShapeDtypeStruct((dm.batch, 8, dm.seq), F32),
        scratch_shapes=[pltpu.VMEM((1, LANE), F32)],
        compiler_params=_params("parallel", "arbitrary"),
        name="fgate",
    )(h, wf, bf)


RESIDUE_STAGES = {4: (4,), 16: (4, 4)}


def _residue_major(ref, src, tmp, dst, stages):
    s1 = stages[0]
    part = ref.shape[0] // s1
    src[...] = ref[...].astype(F32)
    if len(stages) == 1:
        for r1 in range(s1):
            dst[r1] = src[pl.ds(r1, part, stride=s1), :].astype(dst.dtype)
        return
    s2 = stages[1]
    sub = part // s2
    for r1 in range(s1):
        tmp[r1 * part:(r1 + 1) * part, :] = src[pl.ds(r1, part, stride=s1), :]
    for r1 in range(s1):
        for r2 in range(s2):
            dst[r2 * s1 + r1] = tmp[pl.ds(r1 * part + r2, sub, stride=s2), :].astype(dst.dtype)


def _token_order(val, tmp, dst, stages):
    s1 = stages[0]
    part = dst.shape[0] // s1
    if len(stages) == 1:
        for r1 in range(s1):
            dst[pl.ds(r1, part, stride=s1), :] = val[r1]
        return
    s2 = stages[1]
    sub = part // s2
    for r1 in range(s1):
        for r2 in range(s2):
            tmp[pl.ds(r1 * part + r2, sub, stride=s2), :] = val[r2 * s1 + r1]
    for r1 in range(s1):
        dst[pl.ds(r1, part, stride=s1), :] = tmp[r1 * part:(r1 + 1) * part, :]


def _dilated_kernel(q0, k0, v0, q1, k1, v1, q2, k2, v2, gate_ref, o_ref, src, tmp, *scratch, seq):
    nblk = seq // BAND
    scale = HEAD_DIM ** -0.5
    qi = lax.broadcasted_iota(jnp.int32, (BAND, BAND), 0)
    kc = lax.broadcasted_iota(jnp.int32, (BAND, BAND), 1)
    causal = jnp.where(kc <= qi, 0.0, -jnp.inf)
    older = jnp.where(kc >= qi, 0.0, -jnp.inf)
    scratch = list(scratch)

    groups = []
    for refs, (window, dil) in zip(((q0, k0, v0), (q1, k1, v1), (q2, k2, v2)), DILATED_CFG):
        if dil in RESIDUE_STAGES:
            stages = RESIDUE_STAGES[dil]
            q_rm, k_rm, v_rm, o_rm, l_rm, o_tok, l_tok = scratch[:7]
            del scratch[:7]
            for ref, dst in zip(refs, (q_rm, k_rm, v_rm)):
                _residue_major(ref, src, tmp, dst, stages)
            for n in range(seq // dil // BAND):
                cur = slice(n * BAND, (n + 1) * BAND)
                keys = slice(max(0, n - 1) * BAND, (n + 1) * BAND)
                bias = causal if n == 0 else jnp.concatenate([older, causal], axis=1)
                s = jnp.einsum('rqd,rkd->rqk', q_rm[:, cur, :], k_rm[:, keys, :],
                               preferred_element_type=F32) * scale + bias[None]
                m = jnp.max(s, axis=-1, keepdims=True)
                p = jnp.exp(s - m)
                l = jnp.sum(p, axis=-1, keepdims=True)
                o = jnp.einsum('rqk,rkd->rqd', p.astype(BF16), v_rm[:, keys, :],
                               preferred_element_type=F32) / l
                o_rm[:, cur, :] = o
                l_rm[:, cur, :] = jnp.broadcast_to(m + jnp.log(l), o.shape)
            _token_order(o_rm, tmp, o_tok, stages)
            _token_order(l_rm, tmp, l_tok, stages)
            groups.append((o_tok, l_tok))
            continue
        same = ((qi - kc) & (dil - 1)) == 0
        oldest = jnp.where(same & (kc >= qi), 0.0, -jnp.inf)
        middle = jnp.where(same, 0.0, -jnp.inf)
        diag = jnp.where(same & (kc <= qi), 0.0, -jnp.inf)
        groups.append((refs, window // BAND, oldest, middle, diag))

    for n in range(nblk):
        rows = slice(n * BAND, (n + 1) * BAND)
        outs, lses = [], []
        for group in groups:
            if len(group) == 2:
                outs.append(group[0][rows, :])
                lses.append(group[1][rows, :])
                continue
            (q_ref, k_ref, v_ref), wblk, oldest, middle, diag = group
            first = max(0, n - wblk)
            tiles = [oldest if m == n - wblk else middle for m in range(first, n)] + [diag]
            bias = tiles[0] if len(tiles) == 1 else jnp.concatenate(tiles, axis=1)
            keys = slice(first * BAND, (n + 1) * BAND)
            s = lax.dot_general(q_ref[rows, :], k_ref[keys, :], (((1,), (1,)), ((), ())),
                                preferred_element_type=F32) * scale + bias
            m = jnp.max(s, axis=-1, keepdims=True)
            p = jnp.exp(s - m)
            l = jnp.sum(p, axis=-1, keepdims=True)
            outs.append(jnp.dot(p.astype(BF16), v_ref[keys, :], preferred_element_type=F32) / l)
            lses.append(m + jnp.log(l))
        mx = jnp.maximum(jnp.maximum(lses[0], lses[1]), lses[2])
        e = [jnp.exp(a - mx) for a in lses]
        o = (e[0] * outs[0] + e[1] * outs[1] + e[2] * outs[2]) / (e[0] + e[1] + e[2])
        o_ref[rows, :] = (o * _silu(gate_ref[rows, :].astype(F32))).astype(o_ref.dtype)


def _dilated(pg_lo, dm):
    seq = dm.seq
    assert seq % BAND == 0 and all(d & (d - 1) == 0 for _, d in DILATED_CFG)

    def col_spec(cb, g):
        return pl.BlockSpec((seq, HEAD_DIM), lambda b, j, cb=cb, g=g: (b, cb + g * dm.hpg + j))

    in_specs = []
    for g in range(len(DILATED_CFG)):
        in_specs += [col_spec(dm.lo_aq, g), col_spec(dm.lo_ak, g), col_spec(dm.lo_av, g)]
    in_specs.append(pl.BlockSpec((seq, HEAD_DIM), lambda b, j: (b, dm.lo_ag + j)))
    tok = pltpu.VMEM((seq, HEAD_DIM), F32)
    scratch = [tok, tok]
    for _, dil in DILATED_CFG:
        if dil in RESIDUE_STAGES:
            assert math.prod(RESIDUE_STAGES[dil]) == dil and seq % (dil * BAND) == 0
            shape = (dil, seq // dil, HEAD_DIM)
            scratch += [pltpu.VMEM(shape, BF16)] * 3 + [pltpu.VMEM(shape, F32)] * 2 + [tok, tok]
    return pl.pallas_call(
        functools.partial(_dilated_kernel, seq=seq),
        grid=(dm.batch, dm.hpg),
        in_specs=in_specs,
        out_specs=pl.BlockSpec((seq, HEAD_DIM), lambda b, j: (b, j)),
        out_shape=jax.ShapeDtypeStruct((dm.batch * seq, dm.hpg * HEAD_DIM), BF16),
        scratch_shapes=scratch,
        compiler_params=_params("parallel", "parallel"),
        name="dilated_attn",
    )(*([pg_lo] * 10))


def _conv_kernel(a_ref, b_ref, ah_ref, bh_ref, gate_ref, cw_ref, cb_ref, lg_ref, lb_ref,
                 pw_ref, pwb_ref, o_ref, ybuf, zbuf, cbuf, *, tile, rows):
    width = ybuf.shape[1]
    first = pl.program_id(1) == 0
    ybuf[CONV_HALO:CONV_HALO + tile, :] = (
        a_ref[...].astype(F32) * jax.nn.sigmoid(b_ref[...].astype(F32)))
    halo = ah_ref[...].astype(F32) * jax.nn.sigmoid(bh_ref[...].astype(F32))
    ybuf[0:CONV_HALO, :] = jnp.where(first, 0.0, halo)

    total = CONV_HALO + tile
    for c in range(width // LANE):
        cs = slice(c * LANE, (c + 1) * LANE)
        for b in range(1, SUBLANE):
            zbuf[b - 1, SUBLANE:total, :] = ybuf[SUBLANE - b:total - b, cs]
        taps = [cw_ref[j:j + 1, cs] for j in range(CONV_K)]
        bias = cb_ref[:, cs]
        for rc in range(tile // rows):
            acc = jnp.zeros((rows, LANE), F32)
            for lag in range(CONV_K):
                a, b = divmod(lag, SUBLANE)
                start = CONV_HALO + rc * rows - SUBLANE * a
                src = (ybuf[start:start + rows, cs] if b == 0
                       else zbuf[b - 1, start:start + rows, :])
                acc = acc + taps[CONV_K - 1 - lag] * src
            cbuf[rc * rows:(rc + 1) * rows, cs] = acc + bias

    y = cbuf[...]
    mu = jnp.mean(y, axis=-1, keepdims=True)
    yc = y - mu
    var = jnp.mean(yc * yc, axis=-1, keepdims=True)
    yn = yc * lax.rsqrt(var + EPS) * lg_ref[...] + lb_ref[...]
    z = _silu(yn).astype(BF16)
    out = jnp.dot(z, pw_ref[...], preferred_element_type=F32) + pwb_ref[...]
    o_ref[...] = (out * _silu(gate_ref[...].astype(F32))).astype(o_ref.dtype)


def _conv(pg_lo, cw, cb, lg, lb, pw, pwb, dm):
    width = dm.w4 * LANE
    tile = 512
    nt = dm.seq // tile
    hb = tile // CONV_HALO

    def halo_rows(b, i):
        return jnp.maximum((b * nt + i) * hb - 1, 0)

    ca, cbb, cg = dm.lo_ba // dm.w4, dm.lo_bb // dm.w4, dm.lo_bg // dm.w4
    vec = pl.BlockSpec((1, width), lambda b, i: (0, 0))
    return pl.pallas_call(
        functools.partial(_conv_kernel, tile=tile, rows=128),
        grid=(dm.batch, nt),
        in_specs=[pl.BlockSpec((tile, width), lambda b, i: (b * nt + i, ca)),
                  pl.BlockSpec((tile, width), lambda b, i: (b * nt + i, cbb)),
                  pl.BlockSpec((CONV_HALO, width), lambda b, i: (halo_rows(b, i), ca)),
                  pl.BlockSpec((CONV_HALO, width), lambda b, i: (halo_rows(b, i), cbb)),
                  pl.BlockSpec((tile, width), lambda b, i: (b * nt + i, cg)),
                  pl.BlockSpec((CONV_K, width), lambda b, i: (0, 0)),
                  vec, vec, vec,
                  pl.BlockSpec((width, width), lambda b, i: (0, 0)),
                  vec],
        out_specs=pl.BlockSpec((tile, width), lambda b, i: (b * nt + i, 0)),
        out_shape=jax.ShapeDtypeStruct((dm.batch * dm.seq, width), BF16),
        scratch_shapes=[pltpu.VMEM((CONV_HALO + tile, width), F32),
                        pltpu.VMEM((SUBLANE - 1, CONV_HALO + tile, LANE), F32),
                        pltpu.VMEM((tile, width), F32)],
        compiler_params=_params("parallel", "parallel"),
        name="conformer_conv",
    )(pg_lo, pg_lo, pg_lo, pg_lo, pg_lo, cw, cb.reshape(1, width), lg.reshape(1, width),
      lb.reshape(1, width), pw, pwb.reshape(1, width))


def _fox_kernel(q_ref, k_ref, v_ref, gate_ref, c_ref, o_ref, *, tq):
    i = pl.program_id(2)
    q = q_ref[...]
    scale2 = HEAD_DIM ** -0.5 * LOG2E

    def scores(j):
        off = pl.multiple_of(j * tq, tq)
        kb = k_ref[pl.ds(off, tq), :]
        s = lax.dot_general(q, kb, (((1,), (1,)), ((), ())), preferred_element_type=F32)
        return s * scale2 + c_ref[0, :, pl.ds(off, tq)] * (-LOG2E), off

    def update(s, off, carry):
        m, l, acc = carry
        m_new = jnp.maximum(m, jnp.max(s, axis=-1, keepdims=True))
        alpha = jnp.exp2(m - m_new)
        p = jnp.exp2(s - m_new)
        l = alpha * l + jnp.sum(p, axis=-1, keepdims=True)
        vb = v_ref[pl.ds(off, tq), :]
        acc = alpha * acc + jnp.dot(p.astype(BF16), vb, preferred_element_type=F32)
        return m_new, l, acc

    def below_diagonal(j, carry):
        s, off = scores(j)
        return update(s, off, carry)

    init = (jnp.full((tq, 1), -jnp.inf, F32), jnp.zeros((tq, 1), F32),
            jnp.zeros((tq, HEAD_DIM), F32))
    carry = lax.fori_loop(0, i, below_diagonal, init)
    s, off = scores(i)
    row = lax.broadcasted_iota(jnp.int32, (tq, tq), 0)
    col = lax.broadcasted_iota(jnp.int32, (tq, tq), 1)
    _, l, acc = update(jnp.where(col <= row, s, -jnp.inf), off, carry)
    o_ref[...] = (acc / l * _silu(gate_ref[...].astype(F32))).astype(o_ref.dtype)


def _fox(pg_lo, pg_hi, c_rows, dm):
    seq, heads = dm.seq, dm.fox_heads
    tq = 512
    nq = seq // tq
    return pl.pallas_call(
        functools.partial(_fox_kernel, tq=tq),
        grid=(dm.batch, heads, nq),
        in_specs=[pl.BlockSpec((tq, HEAD_DIM), lambda b, h, i: (b * nq + i, dm.lo_cq + h)),
                  pl.BlockSpec((seq, HEAD_DIM), lambda b, h, i: (b, dm.lo_ck + h)),
                  pl.BlockSpec((seq, HEAD_DIM), lambda b, h, i: (b, dm.lo_cv + h)),
                  pl.BlockSpec((tq, HEAD_DIM), lambda b, h, i: (b * nq + i, dm.hi_cg + h)),
                  pl.BlockSpec((1, 1, seq), lambda b, h, i: (b * 8 + h, 0, 0))],
        out_specs=pl.BlockSpec((tq, HEAD_DIM), lambda b, h, i: (b * nq + i, h)),
        out_shape=jax.ShapeDtypeStruct((dm.batch * seq, heads * HEAD_DIM), BF16),
        compiler_params=_params("parallel", "parallel", "parallel"),
        name="fox_attn",
    )(pg_lo, pg_lo, pg_lo, pg_hi, c_rows)


def _pool_kernel(x_ref, xh_ref, gate_ref, pw_ref, ps_ref, o_ref, xbuf, *, tile):
    i = pl.program_id(1)
    xbuf[POOL_HALO:POOL_HALO + tile, :] = x_ref[...].astype(F32)
    xbuf[0:POOL_HALO, :] = jnp.where(i == 0, 0.0, xh_ref[...].astype(F32))
    gc = xbuf.shape[1] // len(POOL_SIZES)
    t = i * tile + lax.broadcasted_iota(jnp.int32, (tile, 1), 0)
    for g, p in enumerate(POOL_SIZES):
        cs = slice(g * gc, (g + 1) * gc)
        x = xbuf[POOL_HALO:POOL_HALO + tile, cs]
        win = x
        for k in range(1, p):
            win = win + xbuf[POOL_HALO - k:POOL_HALO - k + tile, cs]
        cnt = jnp.minimum(t + 1, p).astype(F32)
        d = (win / cnt - x).astype(BF16)
        y = jnp.dot(d, pw_ref[g], preferred_element_type=F32) * ps_ref[:, cs]
        o_ref[:, cs] = (y * _silu(gate_ref[:, cs].astype(F32))).astype(o_ref.dtype)


def _pool(pg_hi, pw, ps, dm):
    width = dm.w4 * LANE
    tile = 512
    nt = dm.seq // tile
    hb = tile // POOL_HALO
    gc = width // len(POOL_SIZES)
    ci, cg = dm.hi_di // dm.w4, dm.hi_dg // dm.w4
    return pl.pallas_call(
        functools.partial(_pool_kernel, tile=tile),
        grid=(dm.batch, nt),
        in_specs=[pl.BlockSpec((tile, width), lambda b, i: (b * nt + i, ci)),
                  pl.BlockSpec((POOL_HALO, width),
                               lambda b, i: (jnp.maximum((b * nt + i) * hb - 1, 0), ci)),
                  pl.BlockSpec((tile, width), lambda b, i: (b * nt + i, cg)),
                  pl.BlockSpec((len(POOL_SIZES), gc, gc), lambda b, i: (0, 0, 0)),
                  pl.BlockSpec((1, width), lambda b, i: (0, 0))],
        out_specs=pl.BlockSpec((tile, width), lambda b, i: (b * nt + i, 0)),
        out_shape=jax.ShapeDtypeStruct((dm.batch * dm.seq, width), BF16),
        scratch_shapes=[pltpu.VMEM((POOL_HALO + tile, width), F32)],
        compiler_params=_params("parallel", "parallel"),
        name="pool_mixer",
    )(pg_hi, pg_hi, pg_hi, pw, ps.reshape(1, width))


def _merge_kernel(ga, gb, gc, gd, ya, yb, yc, yd, wa, wb, wc, wd, o_ref):
    for c in range(o_ref.shape[1] // MXU_COLS):
        cs = slice(c * MXU_COLS, (c + 1) * MXU_COLS)
        acc = None
        for g, y, w in ((ga, ya, wa), (gb, yb, wb), (gc, yc, wc), (gd, yd, wd)):
            u = jnp.dot(y[...], w[:, cs], preferred_element_type=F32)
            term = g[:, cs].astype(F32) * u
            acc = term if acc is None else acc + term
        o_ref[:, cs] = acc.astype(o_ref.dtype)


def _merge(gates, ys, ws, dm):
    m = dm.batch * dm.seq
    d = dm.d_model
    tm = min(1024, m)
    tn = 512
    per_gate = d // tn
    in_specs = [pl.BlockSpec((tm, tn), lambda i, j, k=k: (i, k * per_gate + j))
                for k in range(4)]
    in_specs += [pl.BlockSpec((tm, y.shape[1]), lambda i, j: (i, 0)) for y in ys]
    in_specs += [pl.BlockSpec((w.shape[0], tn), lambda i, j: (0, j)) for w in ws]
    return pl.pallas_call(
        _merge_kernel,
        grid=(m // tm, d // tn),
        in_specs=in_specs,
        out_specs=pl.BlockSpec((tm, tn), lambda i, j: (i, j)),
        out_shape=jax.ShapeDtypeStruct((m, d), BF16),
        compiler_params=_params("parallel", "parallel"),
        name="gated_merge",
    )(gates, gates, gates, gates, *ys, *ws)


def _outproj_kernel(m_ref, w_ref, x_ref, o_ref):
    for c in range(o_ref.shape[1] // MXU_COLS):
        cs = slice(c * MXU_COLS, (c + 1) * MXU_COLS)
        o_ref[:, cs] = x_ref[:, cs] + jnp.dot(m_ref[...], w_ref[:, cs],
                                             preferred_element_type=F32)


def _outproj(merged, w_out, x2):
    m, d = x2.shape
    tm = min(1024, m)
    tn = 512
    return pl.pallas_call(
        _outproj_kernel,
        grid=(m // tm, d // tn),
        in_specs=[pl.BlockSpec((tm, d), lambda i, j: (i, 0)),
                  pl.BlockSpec((d, tn), lambda i, j: (0, j)),
                  pl.BlockSpec((tm, tn), lambda i, j: (i, j))],
        out_specs=pl.BlockSpec((tm, tn), lambda i, j: (i, j)),
        out_shape=jax.ShapeDtypeStruct((m, d), F32),
        compiler_params=_params("parallel", "parallel"),
        name="outproj",
    )(merged, w_out, x2)


def kernel(x, norm_g, w_in, b_forget, conv_w, conv_b, conv_ln_g, conv_ln_b, conv_pw, conv_pw_b,
           pool_w, pool_scale, w_branch, w_out, final_g):
    dm = _dims(x.shape)
    depth = norm_g.shape[0]
    x2 = x.reshape(dm.batch * dm.seq, dm.d_model)
    heads = dm.fox_heads
    cf_col = dm.lo_blocks * LANE
    row_off = (0, dm.hpg * LANE, (dm.hpg + dm.w4) * LANE, (dm.hpg + 2 * dm.w4) * LANE,
               (dm.hpg + 3 * dm.w4) * LANE)

    w_bf = w_in.astype(BF16)
    w_hi = w_bf[:, :, cf_col + heads:]
    for l in range(depth):
        wf = jnp.pad(w_bf[l, :, cf_col:cf_col + heads], ((0, 0), (0, LANE - heads)))
        bf = jnp.pad(b_forget[l], (0, LANE - heads)).reshape(1, LANE)
        ws = [w_branch[l, row_off[k]:row_off[k + 1]].astype(BF16) for k in range(4)]

        h = _rmsnorm(x2, norm_g[l], BF16)
        pg_lo = _inproj(h, w_bf, l, 0, cf_col, False)
        pg_hi = _inproj(h, w_hi, l, 0, dm.hi_gate * LANE, False)
        gates = _inproj(h, w_hi, l, dm.hi_gate * LANE, 4 * dm.d_model, True)
        c_rows = _fgate(h, wf, bf, dm).reshape(dm.batch * 8, 1, dm.seq)
        y_a = _dilated(pg_lo, dm)
        y_b = _conv(pg_lo, conv_w[l], conv_b[l], conv_ln_g[l], conv_ln_b[l],
                    conv_pw[l].astype(BF16), conv_pw_b[l], dm)
        y_c = _fox(pg_lo, pg_hi, c_rows, dm)
        y_d = _pool(pg_hi, pool_w[l].astype(BF16), pool_scale[l], dm)
        merged = _merge(gates, (y_a, y_b, y_c, y_d), ws, dm)
        x2 = _outproj(merged, w_out[l].astype(BF16), x2)

    return _rmsnorm(x2, final_g, x.dtype).reshape(x.shape)
```

```python
import functools
import math
from typing import NamedTuple

import jax
import jax.numpy as jnp
from jax import lax
from jax.experimental import pallas as pl
from jax.experimental.pallas import tpu as pltpu

F32 = jnp.float32
BF16 = jnp.bfloat16

EPS = 1e-6
HEAD_DIM = 128
LANE = 128
SUBLANE = 8
MXU_COLS = 256
DILATED_CFG = ((128, 1), (512, 4), (2048, 16))
BAND = 128
CONV_K = 31
CONV_HALO = 32
POOL_SIZES = (2, 4, 8, 16)
POOL_HALO = 16
VMEM_LIMIT = 56 * 1024 * 1024
LOG2E = math.log2(math.e)


class _Dims(NamedTuple):
    batch: int
    seq: int
    d_model: int
    hpg: int
    w4: int
    fox_heads: int
    lo_aq: int
    lo_ak: int
    lo_av: int
    lo_ag: int
    lo_ba: int
    lo_bb: int
    lo_bg: int
    lo_cq: int
    lo_ck: int
    lo_cv: int
    lo_blocks: int
    hi_cg: int
    hi_di: int
    hi_dg: int
    hi_gate: int
    hi_blocks: int


def _dims(x_shape):
    batch, seq, d = x_shape
    hpg = d // 1024
    w4 = d // 512
    lo_ba = 10 * hpg
    lo_cq = lo_ba + 3 * w4
    return _Dims(batch=batch, seq=seq, d_model=d, hpg=hpg, w4=w4, fox_heads=d // 512,
                 lo_aq=0, lo_ak=3 * hpg, lo_av=6 * hpg, lo_ag=9 * hpg,
                 lo_ba=lo_ba, lo_bb=lo_ba + w4, lo_bg=lo_ba + 2 * w4,
                 lo_cq=lo_cq, lo_ck=lo_cq + w4, lo_cv=lo_cq + 2 * w4, lo_blocks=lo_cq + 3 * w4,
                 hi_cg=0, hi_di=w4, hi_dg=2 * w4, hi_gate=3 * w4,
                 hi_blocks=3 * w4 + 4 * (d // LANE))


def _params(*sem):
    return pltpu.CompilerParams(dimension_semantics=sem, vmem_limit_bytes=VMEM_LIMIT)


def _silu(x):
    return x * jax.nn.sigmoid(x)


def _rmsnorm_kernel(x_ref, g_ref, o_ref):
    x = x_ref[...]
    ms = jnp.mean(x * x, axis=-1, keepdims=True)
    o_ref[...] = (x * lax.rsqrt(ms + EPS) * g_ref[...]).astype(o_ref.dtype)


def _rmsnorm(x2, g, out_dtype):
    m, d = x2.shape
    tm = min(512, m)
    return pl.pallas_call(
        _rmsnorm_kernel,
        grid=(m // tm,),
        in_specs=[pl.BlockSpec((tm, d), lambda i: (i, 0)),
                  pl.BlockSpec((1, d), lambda i: (0, 0))],
        out_specs=pl.BlockSpec((tm, d), lambda i: (i, 0)),
        out_shape=jax.ShapeDtypeStruct((m, d), out_dtype),
        compiler_params=_params("parallel"),
        name="rmsnorm",
    )(x2, g.reshape(1, d))


def _inproj_kernel(h_ref, w_ref, o_ref, *, gate, chunk):
    for c in range(o_ref.shape[1] // chunk):
        cs = slice(c * chunk, (c + 1) * chunk)
        acc = jnp.dot(h_ref[...], w_ref[:, cs], preferred_element_type=F32)
        o_ref[:, cs] = (0.5 * jnp.tanh(0.5 * acc) + 0.5 if gate else acc).astype(o_ref.dtype)


def _col_tile(*aligned):
    return 1024 if all(v % 1024 == 0 for v in aligned) else 512


def _inproj(h, w, layer, col0, n, gate):
    m, d = h.shape
    tm = min(1024, m)
    tn = _col_tile(col0, n)
    assert n % tn == 0 and col0 % tn == 0 and col0 + n <= w.shape[2]
    first = col0 // tn
    return pl.pallas_call(
        functools.partial(_inproj_kernel, gate=gate, chunk=MXU_COLS),
        grid=(m // tm, n // tn),
        in_specs=[pl.BlockSpec((tm, d), lambda i, j: (i, 0)),
                  pl.BlockSpec((None, d, tn), lambda i, j: (layer, 0, first + j))],
        out_specs=pl.BlockSpec((tm, tn), lambda i, j: (i, j)),
        out_shape=jax.ShapeDtypeStruct((m, n), BF16),
        compiler_params=_params("parallel", "parallel"),
        name="inproj_gate" if gate else "inproj",
    )(h, w)


def _fgate_kernel(h_ref, wf_ref, bf_ref, o_ref, carry_ref, *, tile):
    @pl.when(pl.program_id(1) == 0)
    def _():
        carry_ref[...] = jnp.zeros_like(carry_ref)

    f = jnp.dot(h_ref[...], wf_ref[...], preferred_element_type=F32) + bf_ref[...]
    ls = jnp.minimum(f, 0.0) - jnp.log1p(jnp.exp(-jnp.abs(f)))
    row = lax.broadcasted_iota(jnp.int32, (tile, tile), 0)
    col = lax.broadcasted_iota(jnp.int32, (tile, tile), 1)
    tri = jnp.where(row >= col, 1.0, 0.0).astype(BF16)
    hi = ls.astype(BF16)
    r1 = ls - hi.astype(F32)
    mid = r1.astype(BF16)
    lo = (r1 - mid.astype(F32)).astype(BF16)
    c = (jnp.dot(tri, hi, preferred_element_type=F32)
         + jnp.dot(tri, mid, preferred_element_type=F32)
         + jnp.dot(tri, lo, preferred_element_type=F32)) + carry_ref[...]
    carry_ref[...] = c[tile - 1:tile, :]
    o_ref[0] = c.T[0:8, :]


def _fgate(h, wf, bf, dm):
    m, d = h.shape
    tile = 512
    nt = dm.seq // tile
    assert dm.fox_heads <= 8
    return pl.pallas_call(
        functools.partial(_fgate_kernel, tile=tile),
        grid=(dm.batch, nt),
        in_specs=[pl.BlockSpec((tile, d), lambda b, t: (b * nt + t, 0)),
                  pl.BlockSpec((d, LANE), lambda b, t: (0, 0)),
                  pl.BlockSpec((1, LANE), lambda b, t: (0, 0))],
        out_specs=pl.BlockSpec((1, 8, tile), lambda b, t: (b, 0, t)),
        out_shape=jax.ShapeDtypeStruct((dm.batch, 8, dm.seq), F32),
        scratch_shapes=[pltpu.VMEM((1, LANE), F32)],
        compiler_params=_params("parallel", "arbitrary"),
        name="fgate",
    )(h, wf, bf)


RESIDUE_STAGES = {4: (4,), 16: (4, 4)}


def _residue_major(ref, src, tmp, dst, stages):
    s1 = stages[0]
    part = ref.shape[0] // s1
    src[...] = ref[...].astype(F32)
    if len(stages) == 1:
        for r1 in range(s1):
            dst[r1] = src[pl.ds(r1, part, stride=s1), :].astype(dst.dtype)
        return
    s2 = stages[1]
    sub = part // s2
    for r1 in range(s1):
        tmp[r1 * part:(r1 + 1) * part, :] = src[pl.ds(r1, part, stride=s1), :]
    for r1 in range(s1):
        for r2 in range(s2):
            dst[r2 * s1 + r1] = tmp[pl.ds(r1 * part + r2, sub, stride=s2), :].astype(dst.dtype)


def _token_order(val, tmp, dst, stages):
    s1 = stages[0]
    part = dst.shape[0] // s1
    if len(stages) == 1:
        for r1 in range(s1):
            dst[pl.ds(r1, part, stride=s1), :] = val[r1]
        return
    s2 = stages[1]
    sub = part // s2
    for r1 in range(s1):
        for r2 in range(s2):
            tmp[pl.ds(r1 * part + r2, sub, stride=s2), :] = val[r2 * s1 + r1]
    for r1 in range(s1):
        dst[pl.ds(r1, part, stride=s1), :] = tmp[r1 * part:(r1 + 1) * part, :]


def _dilated_kernel(q0, k0, v0, q1, k1, v1, q2, k2, v2, gate_ref, o_ref, src, tmp, *scratch, seq):
    nblk = seq // BAND
    scale = HEAD_DIM ** -0.5
    qi = lax.broadcasted_iota(jnp.int32, (BAND, BAND), 0)
    kc = lax.broadcasted_iota(jnp.int32, (BAND, BAND), 1)
    causal = jnp.where(kc <= qi, 0.0, -jnp.inf)
    older = jnp.where(kc >= qi, 0.0, -jnp.inf)
    scratch = list(scratch)

    groups = []
    for refs, (window, dil) in zip(((q0, k0, v0), (q1, k1, v1), (q2, k2, v2)), DILATED_CFG):
        if dil in RESIDUE_STAGES:
            stages = RESIDUE_STAGES[dil]
            q_rm, k_rm, v_rm, o_rm, l_rm, o_tok, l_tok = scratch[:7]
            del scratch[:7]
            for ref, dst in zip(refs, (q_rm, k_rm, v_rm)):
                _residue_major(ref, src, tmp, dst, stages)
            for n in range(seq // dil // BAND):
                cur = slice(n * BAND, (n + 1) * BAND)
                keys = slice(max(0, n - 1) * BAND, (n + 1) * BAND)
                bias = causal if n == 0 else jnp.concatenate([older, causal], axis=1)
                s = jnp.einsum('rqd,rkd->rqk', q_rm[:, cur, :], k_rm[:, keys, :],
                               preferred_element_type=F32) * scale + bias[None]
                m = jnp.max(s, axis=-1, keepdims=True)
                p = jnp.exp(s - m)
                l = jnp.sum(p, axis=-1, keepdims=True)
                o = jnp.einsum('rqk,rkd->rqd', p.astype(BF16), v_rm[:, keys, :],
                               preferred_element_type=F32) / l
                o_rm[:, cur, :] = o
                l_rm[:, cur, :] = jnp.broadcast_to(m + jnp.log(l), o.shape)
            _token_order(o_rm, tmp, o_tok, stages)
            _token_order(l_rm, tmp, l_tok, stages)
            groups.append((o_tok, l_tok))
            continue
        same = ((qi - kc) & (dil - 1)) == 0
        oldest = jnp.where(same & (kc >= qi), 0.0, -jnp.inf)
        middle = jnp.where(same, 0.0, -jnp.inf)
        diag = jnp.where(same & (kc <= qi), 0.0, -jnp.inf)
        groups.append((refs, window // BAND, oldest, middle, diag))

    for n in range(nblk):
        rows = slice(n * BAND, (n + 1) * BAND)
        outs, lses = [], []
        for group in groups:
            if len(group) == 2:
                outs.append(group[0][rows, :])
                lses.append(group[1][rows, :])
                continue
            (q_ref, k_ref, v_ref), wblk, oldest, middle, diag = group
            first = max(0, n - wblk)
            tiles = [oldest if m == n - wblk else middle for m in range(first, n)] + [diag]
            bias = tiles[0] if len(tiles) == 1 else jnp.concatenate(tiles, axis=1)
            keys = slice(first * BAND, (n + 1) * BAND)
            s = lax.dot_general(q_ref[rows, :], k_ref[keys, :], (((1,), (1,)), ((), ())),
                                preferred_element_type=F32) * scale + bias
            m = jnp.max(s, axis=-1, keepdims=True)
            p = jnp.exp(s - m)
            l = jnp.sum(p, axis=-1, keepdims=True)
            outs.append(jnp.dot(p.astype(BF16), v_ref[keys, :], preferred_element_type=F32) / l)
            lses.append(m + jnp.log(l))
        mx = jnp.maximum(jnp.maximum(lses[0], lses[1]), lses[2])
        e = [jnp.exp(a - mx) for a in lses]
        o = (e[0] * outs[0] + e[1] * outs[1] + e[2] * outs[2]) / (e[0] + e[1] + e[2])
        o_ref[rows, :] = (o * _silu(gate_ref[rows, :].astype(F32))).astype(o_ref.dtype)


def _dilated(pg_lo, dm):
    seq = dm.seq
    assert seq % BAND == 0 and all(d & (d - 1) == 0 for _, d in DILATED_CFG)

    def col_spec(cb, g):
        return pl.BlockSpec((seq, HEAD_DIM), lambda b, j, cb=cb, g=g: (b, cb + g * dm.hpg + j))

    in_specs = []
    for g in range(len(DILATED_CFG)):
        in_specs += [col_spec(dm.lo_aq, g), col_spec(dm.lo_ak, g), col_spec(dm.lo_av, g)]
    in_specs.append(pl.BlockSpec((seq, HEAD_DIM), lambda b, j: (b, dm.lo_ag + j)))
    tok = pltpu.VMEM((seq, HEAD_DIM), F32)
    scratch = [tok, tok]
    for _, dil in DILATED_CFG:
        if dil in RESIDUE_STAGES:
            assert math.prod(RESIDUE_STAGES[dil]) == dil and seq % (dil * BAND) == 0
            shape = (dil, seq // dil, HEAD_DIM)
            scratch += [pltpu.VMEM(shape, BF16)] * 3 + [pltpu.VMEM(shape, F32)] * 2 + [tok, tok]
    return pl.pallas_call(
        functools.partial(_dilated_kernel, seq=seq),
        grid=(dm.batch, dm.hpg),
        in_specs=in_specs,
        out_specs=pl.BlockSpec((seq, HEAD_DIM), lambda b, j: (b, j)),
        out_shape=jax.ShapeDtypeStruct((dm.batch * seq, dm.hpg * HEAD_DIM), BF16),
        scratch_shapes=scratch,
        compiler_params=_params("parallel", "parallel"),
        name="dilated_attn",
    )(*([pg_lo] * 10))


def _conv_kernel(a_ref, b_ref, ah_ref, bh_ref, gate_ref, cw_ref, cb_ref, lg_ref, lb_ref,
                 pw_ref, pwb_ref, o_ref, ybuf, zbuf, cbuf, *, tile, rows):
    width = ybuf.shape[1]
    first = pl.program_id(1) == 0
    ybuf[CONV_HALO:CONV_HALO + tile, :] = (
        a_ref[...].astype(F32) * jax.nn.sigmoid(b_ref[...].astype(F32)))
    halo = ah_ref[...].astype(F32) * jax.nn.sigmoid(bh_ref[...].astype(F32))
    ybuf[0:CONV_HALO, :] = jnp.where(first, 0.0, halo)

    total = CONV_HALO + tile
    for c in range(width // LANE):
        cs = slice(c * LANE, (c + 1) * LANE)
        for b in range(1, SUBLANE):
            zbuf[b - 1, SUBLANE:total, :] = ybuf[SUBLANE - b:total - b, cs]
        taps = [cw_ref[j:j + 1, cs] for j in range(CONV_K)]
        bias = cb_ref[:, cs]
        for rc in range(tile // rows):
            acc = jnp.zeros((rows, LANE), F32)
            for lag in range(CONV_K):
                a, b = divmod(lag, SUBLANE)
                start = CONV_HALO + rc * rows - SUBLANE * a
                src = (ybuf[start:start + rows, cs] if b == 0
                       else zbuf[b - 1, start:start + rows, :])
                acc = acc + taps[CONV_K - 1 - lag] * src
            cbuf[rc * rows:(rc + 1) * rows, cs] = acc + bias

    y = cbuf[...]
    mu = jnp.mean(y, axis=-1, keepdims=True)
    yc = y - mu
    var = jnp.mean(yc * yc, axis=-1, keepdims=True)
    yn = yc * lax.rsqrt(var + EPS) * lg_ref[...] + lb_ref[...]
    z = _silu(yn).astype(BF16)
    out = jnp.dot(z, pw_ref[...], preferred_element_type=F32) + pwb_ref[...]
    o_ref[...] = (out * _silu(gate_ref[...].astype(F32))).astype(o_ref.dtype)


def _conv(pg_lo, cw, cb, lg, lb, pw, pwb, dm):
    width = dm.w4 * LANE
    tile = 512
    nt = dm.seq // tile
    hb = tile // CONV_HALO

    def halo_rows(b, i):
        return jnp.maximum((b * nt + i) * hb - 1, 0)

    ca, cbb, cg = dm.lo_ba // dm.w4, dm.lo_bb // dm.w4, dm.lo_bg // dm.w4
    vec = pl.BlockSpec((1, width), lambda b, i: (0, 0))
    return pl.pallas_call(
        functools.partial(_conv_kernel, tile=tile, rows=128),
        grid=(dm.batch, nt),
        in_specs=[pl.BlockSpec((tile, width), lambda b, i: (b * nt + i, ca)),
                  pl.BlockSpec((tile, width), lambda b, i: (b * nt + i, cbb)),
                  pl.BlockSpec((CONV_HALO, width), lambda b, i: (halo_rows(b, i), ca)),
                  pl.BlockSpec((CONV_HALO, width), lambda b, i: (halo_rows(b, i), cbb)),
                  pl.BlockSpec((tile, width), lambda b, i: (b * nt + i, cg)),
                  pl.BlockSpec((CONV_K, width), lambda b, i: (0, 0)),
                  vec, vec, vec,
                  pl.BlockSpec((width, width), lambda b, i: (0, 0)),
                  vec],
        out_specs=pl.BlockSpec((tile, width), lambda b, i: (b * nt + i, 0)),
        out_shape=jax.ShapeDtypeStruct((dm.batch * dm.seq, width), BF16),
        scratch_shapes=[pltpu.VMEM((CONV_HALO + tile, width), F32),
                        pltpu.VMEM((SUBLANE - 1, CONV_HALO + tile, LANE), F32),
                        pltpu.VMEM((tile, width), F32)],
        compiler_params=_params("parallel", "parallel"),
        name="conformer_conv",
    )(pg_lo, pg_lo, pg_lo, pg_lo, pg_lo, cw, cb.reshape(1, width), lg.reshape(1, width),
      lb.reshape(1, width), pw, pwb.reshape(1, width))


def _fox_kernel(q_ref, k_ref, v_ref, gate_ref, c_ref, o_ref, *, tq):
    i = pl.program_id(2)
    q = q_ref[...]
    scale2 = HEAD_DIM ** -0.5 * LOG2E

    def scores(j):
        off = pl.multiple_of(j * tq, tq)
        kb = k_ref[pl.ds(off, tq), :]
        s = lax.dot_general(q, kb, (((1,), (1,)), ((), ())), preferred_element_type=F32)
        return s * scale2 + c_ref[0, :, pl.ds(off, tq)] * (-LOG2E), off

    def update(s, off, carry):
        m, l, acc = carry
        m_new = jnp.maximum(m, jnp.max(s, axis=-1, keepdims=True))
        alpha = jnp.exp2(m - m_new)
        p = jnp.exp2(s - m_new)
        l = alpha * l + jnp.sum(p, axis=-1, keepdims=True)
        vb = v_ref[pl.ds(off, tq), :]
        acc = alpha * acc + jnp.dot(p.astype(BF16), vb, preferred_element_type=F32)
        return m_new, l, acc

    def below_diagonal(j, carry):
        s, off = scores(j)
        return update(s, off, carry)

    init = (jnp.full((tq, 1), -jnp.inf, F32), jnp.zeros((tq, 1), F32),
            jnp.zeros((tq, HEAD_DIM), F32))
    carry = lax.fori_loop(0, i, below_diagonal, init)
    s, off = scores(i)
    row = lax.broadcasted_iota(jnp.int32, (tq, tq), 0)
    col = lax.broadcasted_iota(jnp.int32, (tq, tq), 1)
    _, l, acc = update(jnp.where(col <= row, s, -jnp.inf), off, carry)
    o_ref[...] = (acc / l * _silu(gate_ref[...].astype(F32))).astype(o_ref.dtype)


def _fox(pg_lo, pg_hi, c_rows, dm):
    seq, heads = dm.seq, dm.fox_heads
    tq = 512
    nq = seq // tq
    return pl.pallas_call(
        functools.partial(_fox_kernel, tq=tq),
        grid=(dm.batch, heads, nq),
        in_specs=[pl.BlockSpec((tq, HEAD_DIM), lambda b, h, i: (b * nq + i, dm.lo_cq + h)),
                  pl.BlockSpec((seq, HEAD_DIM), lambda b, h, i: (b, dm.lo_ck + h)),
                  pl.BlockSpec((seq, HEAD_DIM), lambda b, h, i: (b, dm.lo_cv + h)),
                  pl.BlockSpec((tq, HEAD_DIM), lambda b, h, i: (b * nq + i, dm.hi_cg + h)),
                  pl.BlockSpec((1, 1, seq), lambda b, h, i: (b * 8 + h, 0, 0))],
        out_specs=pl.BlockSpec((tq, HEAD_DIM), lambda b, h, i: (b * nq + i, h)),
        out_shape=jax.ShapeDtypeStruct((dm.batch * seq, heads * HEAD_DIM), BF16),
        compiler_params=_params("parallel", "parallel", "parallel"),
        name="fox_attn",
    )(pg_lo, pg_lo, pg_lo, pg_hi, c_rows)


def _pool_kernel(x_ref, xh_ref, gate_ref, pw_ref, ps_ref, o_ref, xbuf, *, tile):
    i = pl.program_id(1)
    xbuf[POOL_HALO:POOL_HALO + tile, :] = x_ref[...].astype(F32)
    xbuf[0:POOL_HALO, :] = jnp.where(i == 0, 0.0, xh_ref[...].astype(F32))
    gc = xbuf.shape[1] // len(POOL_SIZES)
    t = i * tile + lax.broadcasted_iota(jnp.int32, (tile, 1), 0)
    for g, p in enumerate(POOL_SIZES):
        cs = slice(g * gc, (g + 1) * gc)
        x = xbuf[POOL_HALO:POOL_HALO + tile, cs]
        win = x
        for k in range(1, p):
            win = win + xbuf[POOL_HALO - k:POOL_HALO - k + tile, cs]
        cnt = jnp.minimum(t + 1, p).astype(F32)
        d = (win / cnt - x).astype(BF16)
        y = jnp.dot(d, pw_ref[g], preferred_element_type=F32) * ps_ref[:, cs]
        o_ref[:, cs] = (y * _silu(gate_ref[:, cs].astype(F32))).astype(o_ref.dtype)


def _pool(pg_hi, pw, ps, dm):
    width = dm.w4 * LANE
    tile = 512
    nt = dm.seq // tile
    hb = tile // POOL_HALO
    gc = width // len(POOL_SIZES)
    ci, cg = dm.hi_di // dm.w4, dm.hi_dg // dm.w4
    return pl.pallas_call(
        functools.partial(_pool_kernel, tile=tile),
        grid=(dm.batch, nt),
        in_specs=[pl.BlockSpec((tile, width), lambda b, i: (b * nt + i, ci)),
                  pl.BlockSpec((POOL_HALO, width),
                               lambda b, i: (jnp.maximum((b * nt + i) * hb - 1, 0), ci)),
                  pl.BlockSpec((tile, width), lambda b, i: (b * nt + i, cg)),
                  pl.BlockSpec((len(POOL_SIZES), gc, gc), lambda b, i: (0, 0, 0)),
                  pl.BlockSpec((1, width), lambda b, i: (0, 0))],
        out_specs=pl.BlockSpec((tile, width), lambda b, i: (b * nt + i, 0)),
        out_shape=jax.ShapeDtypeStruct((dm.batch * dm.seq, width), BF16),
        scratch_shapes=[pltpu.VMEM((POOL_HALO + tile, width), F32)],
        compiler_params=_params("parallel", "parallel"),
        name="pool_mixer",
    )(pg_hi, pg_hi, pg_hi, pw, ps.reshape(1, width))


def _merge_kernel(ga, gb, gc, gd, ya, yb, yc, yd, wa, wb, wc, wd, o_ref):
    for c in range(o_ref.shape[1] // MXU_COLS):
        cs = slice(c * MXU_COLS, (c + 1) * MXU_COLS)
        acc = None
        for g, y, w in ((ga, ya, wa), (gb, yb, wb), (gc, yc, wc), (gd, yd, wd)):
            u = jnp.dot(y[...], w[:, cs], preferred_element_type=F32)
            term = g[:, cs].astype(F32) * u
            acc = term if acc is None else acc + term
        o_ref[:, cs] = acc.astype(o_ref.dtype)


def _merge(gates, ys, ws, dm):
    m = dm.batch * dm.seq
    d = dm.d_model
    tm = min(1024, m)
    tn = 512
    per_gate = d // tn
    in_specs = [pl.BlockSpec((tm, tn), lambda i, j, k=k: (i, k * per_gate + j))
                for k in range(4)]
    in_specs += [pl.BlockSpec((tm, y.shape[1]), lambda i, j: (i, 0)) for y in ys]
    in_specs += [pl.BlockSpec((w.shape[0], tn), lambda i, j: (0, j)) for w in ws]
    return pl.pallas_call(
        _merge_kernel,
        grid=(m // tm, d // tn),
        in_specs=in_specs,
        out_specs=pl.BlockSpec((tm, tn), lambda i, j: (i, j)),
        out_shape=jax.ShapeDtypeStruct((m, d), BF16),
        compiler_params=_params("parallel", "parallel"),
        name="gated_merge",
    )(gates, gates, gates, gates, *ys, *ws)


def _outproj_kernel(m_ref, w_ref, x_ref, o_ref):
    for c in range(o_ref.shape[1] // MXU_COLS):
        cs = slice(c * MXU_COLS, (c + 1) * MXU_COLS)
        o_ref[:, cs] = x_ref[:, cs] + jnp.dot(m_ref[...], w_ref[:, cs],
                                             preferred_element_type=F32)


def _outproj(merged, w_out, x2):
    m, d = x2.shape
    tm = min(1024, m)
    tn = min(1024, d)
    return pl.pallas_call(
        _outproj_kernel,
        grid=(m // tm, d // tn),
        in_specs=[pl.BlockSpec((tm, d), lambda i, j: (i, 0)),
                  pl.BlockSpec((d, tn), lambda i, j: (0, j)),
                  pl.BlockSpec((tm, tn), lambda i, j: (i, j))],
        out_specs=pl.BlockSpec((tm, tn), lambda i, j: (i, j)),
        out_shape=jax.ShapeDtypeStruct((m, d), F32),
        compiler_params=_params("parallel", "parallel"),
        name="outproj",
    )(merged, w_out, x2)


def kernel(x, norm_g, w_in, b_forget, conv_w, conv_b, conv_ln_g, conv_ln_b, conv_pw, conv_pw_b,
           pool_w, pool_scale, w_branch, w_out, final_g):
    dm = _dims(x.shape)
    depth = norm_g.shape[0]
    x2 = x.reshape(dm.batch * dm.seq, dm.d_model)
    heads = dm.fox_heads
    cf_col = dm.lo_blocks * LANE
    row_off = (0, dm.hpg * LANE, (dm.hpg + dm.w4) * LANE, (dm.hpg + 2 * dm.w4) * LANE,
               (dm.hpg + 3 * dm.w4) * LANE)

    w_bf = w_in.astype(BF16)
    w_hi = w_bf[:, :, cf_col + heads:]
    for l in range(depth):
        wf = jnp.pad(w_bf[l, :, cf_col:cf_col + heads], ((0, 0), (0, LANE - heads)))
        bf = jnp.pad(b_forget[l], (0, LANE - heads)).reshape(1, LANE)
        ws = [w_branch[l, row_off[k]:row_off[k + 1]].astype(BF16) for k in range(4)]

        h = _rmsnorm(x2, norm_g[l], BF16)
        pg_lo = _inproj(h, w_bf, l, 0, cf_col, False)
        pg_hi = _inproj(h, w_hi, l, 0, dm.hi_gate * LANE, False)
        gates = _inproj(h, w_hi, l, dm.hi_gate * LANE, 4 * dm.d_model, True)
        c_rows = _fgate(h, wf, bf, dm).reshape(dm.batch * 8, 1, dm.seq)
        y_a = _dilated(pg_lo, dm)
        y_b = _conv(pg_lo, conv_w[l], conv_b[l], conv_ln_g[l], conv_ln_b[l],
                    conv_pw[l].astype(BF16), conv_pw_b[l], dm)
        y_c = _fox(pg_lo, pg_hi, c_rows, dm)
        y_d = _pool(pg_hi, pool_w[l].astype(BF16), pool_scale[l], dm)
        merged = _merge(gates, (y_a, y_b, y_c, y_d), ws, dm)
        x2 = _outproj(merged, w_out[l].astype(BF16), x2)

    return _rmsnorm(x2, final_g, x.dtype).reshape(x.shape)
```
